```python
import math
import jax
import jax.numpy as jnp
from jax import lax
import numpy as np

D_MODEL = 2048
BATCH = 16
SEQ = 256
DEPTH = 2
DEC_BATCH = 4
DEC_SEQ = 4096
PAST_LEN = 512

GRID_W = 64
EPS = 1e-6
Q_BLOCK = 128
NH_A = 4
DQK_A = 128
DV_A = 256
A_QK = NH_A * DQK_A
A_V = NH_A * DV_A
N_GATE = 4 * NH_A
CHUNK = 64
GATE_CAP = 15.0
NH_B = 8
Q_LORA = 512
KV_LORA = 512
NOPE = 128
ROPE = 64
VH = 128
ROPE_BASE = 10000.0
IN0_COLS = 2 * A_QK + 2 * A_V + N_GATE + Q_LORA + KV_LORA + ROPE
MIX0_OUT = A_V + NH_B * VH
NH_C = 16
DH_C = 128
WIN_R = 8
WIN_C = 16
N_EXP = 32
TOP_K = 4
D_FF = 2048
SWIGLU_ALPHA = 1.702
SWIGLU_LIMIT = 7.0
MOE_BLOCK = 128

kernel_name = 'hybrid_mlstm_mla_natten_moe_flow_step'


def rmsnorm(x, g):
    xf = x.astype(jnp.float32)
    y = xf * lax.rsqrt(jnp.mean(xf * xf, axis=-1, keepdims=True) + EPS)
    return (y * g.astype(jnp.float32)).astype(x.dtype)


def axial_rotary(x):
    T = x.shape[1]
    half = ROPE // 2
    quarter = half // 2
    inv = ROPE_BASE ** (-jnp.arange(quarter, dtype=jnp.float32) * 2.0 / half)
    t = jnp.arange(T)
    xf = x.astype(jnp.float32)

    def rot(seg, pos):
        ang = pos.astype(jnp.float32)[:, None] * inv
        ang = ang.reshape((T,) + (1,) * (seg.ndim - 3) + (quarter,))
        cos, sin = jnp.cos(ang), jnp.sin(ang)
        s1, s2 = seg[..., :quarter], seg[..., quarter:]
        return jnp.concatenate([s1 * cos - s2 * sin, s1 * sin + s2 * cos], axis=-1)

    out = jnp.concatenate([rot(xf[..., :half], t // GRID_W), rot(xf[..., half:], t % GRID_W)], axis=-1)
    return out.astype(x.dtype)


def rotary_tail(x):
    return jnp.concatenate([x[..., :NOPE], axial_rotary(x[..., NOPE:])], axis=-1)


def dense_attention(q, k, v, scale):
    B, Tq, H, D = q.shape
    nb = Tq // Q_BLOCK
    qb = jnp.moveaxis(q.reshape(B, nb, Q_BLOCK, H, D), 1, 0)

    def one(qi):
        s = jnp.einsum('bqhd,bkhd->bhqk', qi, k).astype(jnp.float32) * scale
        p = jax.nn.softmax(s, axis=-1).astype(v.dtype)
        return jnp.einsum('bhqk,bkhd->bqhd', p, v)

    o = lax.map(one, qb)
    return jnp.moveaxis(o, 0, 1).reshape(B, Tq, H, v.shape[-1])


def mlstm_chunked(q, k, v, ig, lf, C0, n0, m0):
    B, H, T, _ = q.shape
    nc = T // CHUNK

    def chunks(a):
        return jnp.moveaxis(a.reshape((B, H, nc, CHUNK) + a.shape[3:]), 2, 0)

    causal = jnp.tril(jnp.ones((CHUNK, CHUNK), dtype=bool))

    def step(carry, inp):
        C, n, m = carry
        qc, kc, vc, ic, fc = inp
        b = jnp.cumsum(fc, axis=-1)
        dmat = jnp.where(causal, b[..., :, None] - b[..., None, :] + ic[..., None, :], -jnp.inf)
        inter = b + m[..., None]
        m_t = jnp.maximum(inter, jnp.max(dmat, axis=-1))
        a = jnp.exp(inter - m_t)
        s = jnp.einsum('bhtd,bhsd->bhts', qc, kc) * jnp.exp(dmat - m_t[..., None])
        num = a[..., None] * jnp.einsum('bhtd,bhde->bhte', qc, C) + jnp.einsum('bhts,bhse->bhte', s, vc)
        den = a * jnp.einsum('bhtd,bhd->bht', qc, n) + jnp.sum(s, axis=-1)
        h = num / jnp.maximum(jnp.abs(den), jnp.exp(-m_t))[..., None]
        b_end = b[..., -1]
        g = b_end[..., None] - b + ic
        m_new = jnp.maximum(b_end + m, jnp.max(g, axis=-1))
        a_s = jnp.exp(b_end + m - m_new)
        w_s = jnp.exp(g - m_new[..., None])
        kw = kc * w_s[..., None]
        C_new = a_s[..., None, None] * C + jnp.einsum('bhsd,bhse->bhde', kw, vc)
        n_new = a_s[..., None] * n + jnp.sum(kw, axis=2)
        return (C_new, n_new, m_new), h

    (C, n, m), h = lax.scan(step, (C0, n0, m0), (chunks(q), chunks(k), chunks(v), chunks(ig), chunks(lf)))
    h = jnp.moveaxis(h, 0, 2).reshape(B, H, T, v.shape[-1])
    return h, (C, n, m)


def mla_kv(ckv, kpe, w_kvb, g_kn):
    B, T, _ = ckv.shape
    kv = (ckv @ w_kvb).reshape(B, T, NH_B, NOPE + VH)
    k = jnp.concatenate([kv[..., :NOPE], jnp.broadcast_to(kpe[:, :, None, :], (B, T, NH_B, ROPE)).astype(kv.dtype)], axis=-1)
    return rmsnorm(k, g_kn), kv[..., NOPE:]


def mixer_ab(h, params, cache):
    w_in, b_gate, g_qa, w_qb, g_kva, w_kvb, g_qn, g_kn, g_hn, w_out = params
    B, T, _ = h.shape
    sizes = [A_QK, A_QK, A_V, A_V, N_GATE, Q_LORA, KV_LORA, ROPE]
    cuts = [int(s) for s in np.cumsum(sizes)[:-1]]
    q_a, k_a, v_a, o_a, gates, cq, ckv, kpe = jnp.split(h @ w_in, cuts, axis=-1)

    def heads(a, d):
        return a.reshape(B, T, NH_A, d).transpose(0, 2, 1, 3).astype(jnp.float32)

    qA = heads(q_a, DQK_A)
    kA = heads(k_a, DQK_A) * (DQK_A ** -0.5)
    vA = heads(v_a, DV_A)
    gt = (gates + b_gate).astype(jnp.float32)
    gt = GATE_CAP * jnp.tanh(gt / GATE_CAP)
    gt = gt.reshape(B, T, 4, NH_A).transpose(2, 0, 3, 1)
    if cache is None:
        zC = jnp.zeros((B, NH_A, DQK_A, DV_A), jnp.float32)
        zn = jnp.zeros((B, NH_A, DQK_A), jnp.float32)
        zm = jnp.zeros((B, NH_A), jnp.float32)
        st_fw = (zC, zn, zm)
        st_bw = (zC, zn, zm)
    else:
        sC = cache[0].astype(jnp.float32)
        sn = cache[1].astype(jnp.float32)
        sm = cache[2].astype(jnp.float32)
        st_fw = (sC[:, 0], sn[:, 0], sm[:, 0])
        st_bw = (sC[:, 1], sn[:, 1], sm[:, 1])
    h_fw, fin_fw = mlstm_chunked(qA, kA, vA, gt[0], jax.nn.log_sigmoid(gt[1]), *st_fw)

    def flip(a):
        return jnp.flip(a, axis=2)

    h_bw, fin_bw = mlstm_chunked(flip(qA), flip(kA), flip(vA), flip(gt[2]), flip(jax.nn.log_sigmoid(gt[3])), *st_bw)
    hA = rmsnorm((h_fw + flip(h_bw)).transpose(0, 2, 1, 3), g_hn.reshape(NH_A, DV_A)).reshape(B, T, A_V)
    yA = (jax.nn.sigmoid(o_a.astype(jnp.float32)) * hA).astype(h.dtype)

    qB = rmsnorm((rmsnorm(cq, g_qa) @ w_qb).reshape(B, T, NH_B, NOPE + ROPE), g_qn)
    ckv = rmsnorm(ckv, g_kva)
    kB, vB = mla_kv(ckv, kpe, w_kvb, g_kn)
    if cache is None:
        new = (jnp.stack([fin_fw[0], fin_bw[0]], axis=1), jnp.stack([fin_fw[1], fin_bw[1]], axis=1),
               jnp.stack([fin_fw[2], fin_bw[2]], axis=1), ckv, kpe)
    else:
        qB = rotary_tail(qB)
        kB = rotary_tail(kB)
        k_ctx, v_ctx = mla_kv(cache[3].astype(h.dtype), cache[4].astype(h.dtype), w_kvb, g_kn)
        kB = jnp.concatenate([kB, k_ctx], axis=1)
        vB = jnp.concatenate([vB, v_ctx], axis=1)
        new = None
    yB = dense_attention(qB, kB, vB, (NOPE + ROPE) ** -0.5).reshape(B, T, NH_B * VH)
    return jnp.concatenate([yA, yB], axis=-1) @ w_out, new


def neighbourhood_attention(q, k, v, k_ctx, v_ctx, rpb):
    B, T, H, Dh = q.shape
    rows = T // GRID_W
    wr = min(WIN_R, rows)
    ncb = GRID_W // WIN_C
    band = 2 * WIN_C
    qg = q.reshape(B, rows, ncb, WIN_C, H, Dh)
    kg = k.reshape(B, rows, GRID_W, H, Dh)
    vg = v.reshape(B, rows, GRID_W, H, Dh)
    qcol = np.arange(GRID_W).reshape(ncb, WIN_C)
    bstart = np.clip(np.arange(ncb) * WIN_C - WIN_C // 2, 0, GRID_W - band)
    bcol = bstart[:, None] + np.arange(band)
    cstart = np.clip(qcol - WIN_C // 2, 0, GRID_W - WIN_C)
    col_ok = (bcol[:, None, :] >= cstart[:, :, None]) & (bcol[:, None, :] < cstart[:, :, None] + WIN_C)
    dc_idx = np.clip(bcol[:, None, :] - qcol[:, :, None] + WIN_C - 1, 0, 2 * WIN_C - 2)
    scale = DH_C ** -0.5

    def row(r):
        rs = jnp.clip(r - wr // 2, 0, rows - wr)
        k_r = lax.dynamic_slice_in_dim(kg, rs, wr, axis=1)[:, :, bcol]
        v_r = lax.dynamic_slice_in_dim(vg, rs, wr, axis=1)[:, :, bcol]
        q_r = lax.dynamic_index_in_dim(qg, r, axis=1, keepdims=False)
        dr_idx = rs + jnp.arange(wr) - r + WIN_R - 1
        bias = rpb[:, dr_idx[None, None, :, None], dc_idx[:, :, None, :]].astype(jnp.float32)
        s_win = jnp.einsum('bnchd,bjnkhd->bhncjk', q_r, k_r).astype(jnp.float32) * scale + bias[None]
        s_win = jnp.where(col_ok[None, None, :, :, None, :], s_win, -jnp.inf)
        s_ctx = jnp.einsum('bnchd,bphd->bhncp', q_r, k_ctx).astype(jnp.float32) * scale
        s = jnp.concatenate([s_win.reshape(B, H, ncb, WIN_C, wr * band), s_ctx], axis=-1)
        p = jax.nn.softmax(s, axis=-1).astype(v.dtype)
        p_win = p[..., :wr * band].reshape(B, H, ncb, WIN_C, wr, band)
        p_ctx = p[..., wr * band:]
        return jnp.einsum('bhncjk,bjnkhd->bnchd', p_win, v_r) + jnp.einsum('bhncp,bphd->bnchd', p_ctx, v_ctx)

    o = lax.map(row, jnp.arange(rows))
    return jnp.moveaxis(o, 0, 1).reshape(B, T, H, Dh)


def mixer_c(h, params, cache):
    w_qkv, g_qn, g_kn, rpb, w_out = params
    B, T, _ = h.shape
    qkv = (h @ w_qkv).reshape(B, T, 3, NH_C, DH_C)
    q = rmsnorm(qkv[:, :, 0], g_qn)
    k = rmsnorm(qkv[:, :, 1], g_kn)
    v = qkv[:, :, 2]
    if cache is None:
        o = dense_attention(q, k, v, DH_C ** -0.5)
        new = (k, v)
    else:
        o = neighbourhood_attention(q, k, v, cache[0].astype(q.dtype), cache[1].astype(v.dtype), rpb)
        new = None
    return o.reshape(B, T, NH_C * DH_C) @ w_out, new


def moe(h, w_r, b_r, w1, b1, w2, b2):
    B, T, D = h.shape
    x = h.reshape(-1, D)
    N = x.shape[0]
    NK = N * TOP_K
    logits = (x @ w_r + b_r).astype(jnp.float32)
    top_v, top_i = lax.top_k(logits, TOP_K)
    gate = jax.nn.softmax(top_v, axis=-1)
    e = top_i.reshape(-1).astype(jnp.int32)
    tok = jnp.arange(NK, dtype=jnp.int32) // TOP_K
    g = gate.reshape(-1)
    counts = jnp.bincount(e, length=N_EXP)
    padded = (counts + MOE_BLOCK - 1) // MOE_BLOCK * MOE_BLOCK
    pad_end = jnp.cumsum(padded)
    pad_start = pad_end - padded
    raw_start = jnp.cumsum(counts) - counts
    order = jnp.argsort(e)
    se = e[order]
    dest = pad_start[se] + jnp.arange(NK, dtype=jnp.int32) - raw_start[se]
    n_blocks = (NK + N_EXP * (MOE_BLOCK - 1) + MOE_BLOCK - 1) // MOE_BLOCK
    P = n_blocks * MOE_BLOCK
    slot_tok = jnp.full((P,), N, dtype=jnp.int32).at[dest].set(tok[order])
    slot_g = jnp.zeros((P,), jnp.float32).at[dest].set(g[order])
    blk_e = jnp.minimum(jnp.searchsorted(pad_end, jnp.arange(n_blocks, dtype=jnp.int32) * MOE_BLOCK, side='right'), N_EXP - 1)
    xp = jnp.concatenate([x, jnp.zeros((1, D), x.dtype)], axis=0)
    xb = xp[slot_tok].reshape(n_blocks, MOE_BLOCK, D)

    def expert_block(args):
        xi, ei = args
        hh = xi @ w1[ei] + b1[ei]
        glu = jnp.minimum(hh[..., ::2], SWIGLU_LIMIT)
        lin = jnp.clip(hh[..., 1::2], -SWIGLU_LIMIT, SWIGLU_LIMIT)
        act = glu * jax.nn.sigmoid(SWIGLU_ALPHA * glu) * (lin + 1.0)
        return act @ w2[ei] + b2[ei]

    yb = lax.map(expert_block, (xb, blk_e)).reshape(P, D)
    y = jax.ops.segment_sum(yb * slot_g[:, None].astype(yb.dtype), slot_tok, num_segments=N + 1)[:N]
    return y.reshape(B, T, D)


def setup_inputs(seed: int = 0) -> dict:
    key = jax.random.key(seed)
    ks = iter(jax.random.split(key, 64))
    D = D_MODEL

    def nrm(shape, s=1.0):
        return jax.random.normal(next(ks), shape, jnp.float32) * s

    def gain(n):
        return 1.0 + nrm((n,), 0.01)

    def gate_bias():
        f_off = jnp.linspace(3.0, 6.0, NH_A, dtype=jnp.float32)
        return jnp.concatenate([nrm((NH_A,), 0.1), f_off + nrm((NH_A,), 0.1), nrm((NH_A,), 0.1), f_off + nrm((NH_A,), 0.1)])

    inp = {}
    inp['x_prompt'] = nrm((BATCH, SEQ, D))
    inp['x_sample'] = nrm((DEC_BATCH, DEC_SEQ, D))
    inp['state_l0_mlstm_C'] = nrm((DEC_BATCH, 2, NH_A, DQK_A, DV_A))
    inp['state_l0_mlstm_n'] = nrm((DEC_BATCH, 2, NH_A, DQK_A))
    inp['state_l0_mlstm_m'] = nrm((DEC_BATCH, 2, NH_A), 0.5)
    inp['cache_l0_mla_ckv'] = nrm((DEC_BATCH, PAST_LEN, KV_LORA))
    inp['cache_l0_mla_kpe'] = nrm((DEC_BATCH, PAST_LEN, ROPE))
    inp['cache_l1_na_k'] = nrm((DEC_BATCH, PAST_LEN, NH_C, DH_C))
    inp['cache_l1_na_v'] = nrm((DEC_BATCH, PAST_LEN, NH_C, DH_C))
    inp['c'] = nrm((DEC_BATCH, D))
    inp['c_ctx'] = nrm((D,))
    inp['l0_g_mix'] = gain(D)
    inp['l0_g_moe'] = gain(D)
    inp['l0_ada_w'] = nrm((D, 6 * D), 0.5 * D ** -0.5)
    inp['l0_ada_b'] = nrm((6 * D,), 0.02)
    inp['l0_w_in'] = nrm((D, IN0_COLS), D ** -0.5)
    inp['l0_b_gate'] = gate_bias()
    inp['l0_g_qa'] = gain(Q_LORA)
    inp['l0_w_qb'] = nrm((Q_LORA, NH_B * (NOPE + ROPE)), Q_LORA ** -0.5)
    inp['l0_g_kva'] = gain(KV_LORA)
    inp['l0_w_kvb'] = nrm((KV_LORA, NH_B * (NOPE + VH)), KV_LORA ** -0.5)
    inp['l0_g_qn'] = gain(NOPE + ROPE)
    inp['l0_g_kn'] = gain(NOPE + ROPE)
    inp['l0_g_hn'] = gain(A_V)
    inp['l0_w_out'] = nrm((MIX0_OUT, D), MIX0_OUT ** -0.5)
    inp['l0_w_router'] = nrm((D, N_EXP), D ** -0.5)
    inp['l0_b_router'] = nrm((N_EXP,), 0.01)
    inp['l0_w1'] = nrm((N_EXP, D, 2 * D_FF), D ** -0.5)
    inp['l0_b1'] = nrm((N_EXP, 2 * D_FF), 0.01)
    inp['l0_w2'] = nrm((N_EXP, D_FF, D), D_FF ** -0.5)
    inp['l0_b2'] = nrm((N_EXP, D), 0.01)
    inp['l1_g_mix'] = gain(D)
    inp['l1_g_moe'] = gain(D)
    inp['l1_ada_w'] = nrm((D, 6 * D), 0.5 * D ** -0.5)
    inp['l1_ada_b'] = nrm((6 * D,), 0.02)
    inp['l1_w_qkv'] = nrm((D, 3 * NH_C * DH_C), D ** -0.5)
    inp['l1_g_qn'] = gain(DH_C)
    inp['l1_g_kn'] = gain(DH_C)
    inp['l1_rpb'] = nrm((NH_C, 2 * WIN_R - 1, 2 * WIN_C - 1), 0.1)
    inp['l1_w_out'] = nrm((NH_C * DH_C, D), (NH_C * DH_C) ** -0.5)
    inp['l1_w_router'] = nrm((D, N_EXP), D ** -0.5)
    inp['l1_b_router'] = nrm((N_EXP,), 0.01)
    inp['l1_w1'] = nrm((N_EXP, D, 2 * D_FF), D ** -0.5)
    inp['l1_b1'] = nrm((N_EXP, 2 * D_FF), 0.01)
    inp['l1_w2'] = nrm((N_EXP, D_FF, D), D_FF ** -0.5)
    inp['l1_b2'] = nrm((N_EXP, D), 0.01)
    return inp


def reference(x_prompt, x_sample, state_l0_mlstm_C, state_l0_mlstm_n, state_l0_mlstm_m,
              cache_l0_mla_ckv, cache_l0_mla_kpe, cache_l1_na_k, cache_l1_na_v, c, c_ctx,
              l0_g_mix, l0_g_moe, l0_ada_w, l0_ada_b, l0_w_in, l0_b_gate, l0_g_qa, l0_w_qb,
              l0_g_kva, l0_w_kvb, l0_g_qn, l0_g_kn, l0_g_hn, l0_w_out, l0_w_router, l0_b_router,
              l0_w1, l0_b1, l0_w2, l0_b2,
              l1_g_mix, l1_g_moe, l1_ada_w, l1_ada_b, l1_w_qkv, l1_g_qn, l1_g_kn, l1_rpb, l1_w_out,
              l1_w_router, l1_b_router, l1_w1, l1_b1, l1_w2, l1_b2):
    mix_params = [
        (l0_w_in, l0_b_gate, l0_g_qa, l0_w_qb, l0_g_kva, l0_w_kvb, l0_g_qn, l0_g_kn, l0_g_hn, l0_w_out),
        (l1_w_qkv, l1_g_qn, l1_g_kn, l1_rpb, l1_w_out),
    ]
    common = [
        (l0_g_mix, l0_g_moe, l0_ada_w, l0_ada_b, l0_w_router, l0_b_router, l0_w1, l0_b1, l0_w2, l0_b2),
        (l1_g_mix, l1_g_moe, l1_ada_w, l1_ada_b, l1_w_router, l1_b_router, l1_w1, l1_b1, l1_w2, l1_b2),
    ]
    caches = [
        (state_l0_mlstm_C, state_l0_mlstm_n, state_l0_mlstm_m, cache_l0_mla_ckv, cache_l0_mla_kpe),
        (cache_l1_na_k, cache_l1_na_v),
    ]

    def run_layer(x, cond, layer, cache):
        g_mix, g_moe, ada_w, ada_b, w_r, b_r, w1, b1, w2, b2 = common[layer]
        mod = (jax.nn.silu(cond) @ ada_w + ada_b).astype(x.dtype)
        if cond.ndim == 2:
            mod = mod[:, None, :]
        sh1, sc1, ga1, sh2, sc2, ga2 = jnp.split(mod, 6, axis=-1)
        hdn = rmsnorm(x, g_mix) * (1 + sc1) + sh1
        mixer = mixer_ab if layer % 2 == 0 else mixer_c
        mix, new = mixer(hdn, mix_params[layer], cache)
        x = x + ga1 * mix
        hdn = rmsnorm(x, g_moe) * (1 + sc2) + sh2
        x = x + ga2 * moe(hdn, w_r, b_r, w1, b1, w2, b2)
        return x, new

    y_prompt = x_prompt
    new_states = []
    for layer in range(DEPTH):
        y_prompt, new = run_layer(y_prompt, c_ctx, layer, None)
        new_states.append(new)
    y_sample = x_sample
    for layer in range(DEPTH):
        y_sample, _ = run_layer(y_sample, c, layer, caches[layer])
    (new_C, new_n, new_m, new_ckv, new_kpe), (new_k, new_v) = new_states
    return (y_prompt, y_sample, new_C, new_n, new_m, new_ckv, new_kpe, new_k, new_v)
```

```python
import functools

import numpy as np
import jax
import jax.numpy as jnp
from jax import lax
from jax.experimental import pallas as pl
from jax.experimental.pallas import tpu as pltpu

BF = jnp.bfloat16
F32 = jnp.float32

D_MODEL = 2048
N_CTX_B, CTX_T = 16, 256
N_LAT_B, LAT_T = 4, 4096
N_CTX = N_CTX_B * CTX_T
N_LAT = N_LAT_B * LAT_T
N_TOK = N_CTX + N_LAT
PAST = 512
GRID_W = 64
GRID_ROWS = LAT_T // GRID_W
EPS = 1e-6
NEG = -1e30

NH_A, DQK_A, DV_A = 4, 128, 256
A_QK, A_V = NH_A * DQK_A, NH_A * DV_A
N_GATE = 4 * NH_A
CHUNK = 64
GATE_CAP = 15.0
NH_B, Q_LORA, KV_LORA, NOPE, ROPE, VH = 8, 512, 512, 128, 64, 128
QK_B = NOPE + ROPE
QK_B_PAD = 256
ROPE_BASE = 10000.0
NH_C, DH_C = 16, 128
WIN_R, WIN_C = 8, 16
NA_ROWS = 4
NA_KROWS = 12
N_EXP, TOP_K, D_FF = 32, 4, 2048
SWIGLU_ALPHA, SWIGLU_LIMIT = 1.702, 7.0
MOE_BM = 512
MOE_TF = 512
MOE_NB = (N_TOK * TOP_K + N_EXP * (MOE_BM - 1) + MOE_BM - 1) // MOE_BM
AUX_W = 128
MLA_IN_W = 1280

VMEM_LIMIT = 56 * 1024 * 1024


def _cparams(sem):
    return pltpu.CompilerParams(dimension_semantics=sem, vmem_limit_bytes=VMEM_LIMIT)


def _split_bf16(x):
    hi = x.astype(BF)
    lo = (x - hi.astype(F32)).astype(BF)
    return hi, lo


def _norm_mod_kernel(*refs, which, has_aux):
    if has_aux:
        x_ref, g_ref, mod_ref, w_ref, b_ref, o_ref, aux_ref = refs
    else:
        x_ref, g_ref, mod_ref, o_ref = refs
    x = x_ref[...]
    ms = jnp.mean(x * x, axis=-1, keepdims=True)
    y = x * lax.rsqrt(ms + EPS) * g_ref[...]
    sh = mod_ref[0, 3 * which:3 * which + 1, :]
    sc = mod_ref[0, 3 * which + 1:3 * which + 2, :]
    h = y * (1.0 + sc) + sh
    o_ref[...] = h.astype(BF)
    if has_aux:
        h_hi, h_lo = _split_bf16(h)
        w_hi, w_lo = _split_bf16(w_ref[...])
        acc = jnp.dot(h_hi, w_hi, preferred_element_type=F32)
        acc += jnp.dot(h_hi, w_lo, preferred_element_type=F32)
        acc += jnp.dot(h_lo, w_hi, preferred_element_type=F32)
        aux_ref[...] = acc + b_ref[...]


def norm_mod(x, g, mod, which, aux_w=None, aux_b=None, tm=512):
    n, d = x.shape
    has_aux = aux_w is not None
    in_specs = [
        pl.BlockSpec((tm, d), lambda i: (i, 0)),
        pl.BlockSpec((1, d), lambda i: (0, 0)),
        pl.BlockSpec((1, 6, d), lambda i: ((i * tm) // N_CTX, 0, 0)),
    ]
    args = [x, g.reshape(1, d), mod]
    out_shape = [jax.ShapeDtypeStruct((n, d), BF)]
    out_specs = [pl.BlockSpec((tm, d), lambda i: (i, 0))]
    if has_aux:
        in_specs += [pl.BlockSpec((d, AUX_W), lambda i: (0, 0)), pl.BlockSpec((1, AUX_W), lambda i: (0, 0))]
        args += [aux_w, aux_b.reshape(1, AUX_W)]
        out_shape.append(jax.ShapeDtypeStruct((n, AUX_W), F32))
        out_specs.append(pl.BlockSpec((tm, AUX_W), lambda i: (i, 0)))
    res = pl.pallas_call(
        functools.partial(_norm_mod_kernel, which=which, has_aux=has_aux),
        grid=(n // tm,), in_specs=in_specs, out_specs=out_specs, out_shape=out_shape,
        compiler_params=_cparams(("parallel",)), name="norm_mod",
    )(*args)
    return res if has_aux else res[0]


def _group_rmsnorm(acc, gain, group, div):
    parts = []
    for c in range(acc.shape[1] // group):
        blk = acc[:, c * group:(c + 1) * group]
        ms = jnp.sum(blk * blk, axis=-1, keepdims=True) * (1.0 / div)
        parts.append(blk * lax.rsqrt(ms + EPS) * gain[:, c * group:(c + 1) * group])
    return parts[0] if len(parts) == 1 else jnp.concatenate(parts, axis=-1)


def _mm_kernel(*refs, has_bias, gate_row, norm_group, norm_div, n_norm_tiles):
    a_ref, w_ref = refs[0], refs[1]
    pos = 2
    if has_bias:
        b_ref = refs[pos]; pos += 1
    if gate_row is not None:
        x_ref, mod_ref = refs[pos], refs[pos + 1]; pos += 2
    if norm_group:
        gain_ref = refs[pos]; pos += 1
    o_ref = refs[pos]

    acc = jnp.dot(a_ref[...].astype(BF), w_ref[...].astype(BF), preferred_element_type=F32)
    if has_bias:
        acc = acc + b_ref[...]
    if gate_row is not None:
        acc = x_ref[...] + mod_ref[0, gate_row:gate_row + 1, :] * acc
    if norm_group:
        j = pl.program_id(1)

        @pl.when(j < n_norm_tiles)
        def _():
            o_ref[...] = _group_rmsnorm(acc, gain_ref[...], norm_group, norm_div).astype(o_ref.dtype)

        @pl.when(j >= n_norm_tiles)
        def _():
            o_ref[...] = acc.astype(o_ref.dtype)
    else:
        o_ref[...] = acc.astype(o_ref.dtype)


def matmul(a, w, *, bias=None, resid=None, mod=None, gate_row=None, gain=None, norm_group=0,
           norm_div=1.0, n_norm_cols=None, out_dtype=F32, tm=1024, tn=512):
    m, k = a.shape
    n = w.shape[1]
    tm = min(tm, m)
    tn = min(tn, n)
    assert m % tm == 0 and n % tn == 0
    in_specs = [pl.BlockSpec((tm, k), lambda i, j: (i, 0)), pl.BlockSpec((k, tn), lambda i, j: (0, j))]
    args = [a, w]
    if bias is not None:
        in_specs.append(pl.BlockSpec((1, tn), lambda i, j: (0, j)))
        args.append(bias.reshape(1, n))
    if gate_row is not None:
        in_specs.append(pl.BlockSpec((tm, tn), lambda i, j: (i, j)))
        in_specs.append(pl.BlockSpec((1, 6, tn), lambda i, j: ((i * tm) // N_CTX, 0, j)))
        args += [resid, mod]
    n_norm_tiles = 0
    if norm_group:
        n_norm_cols = n if n_norm_cols is None else n_norm_cols
        assert n_norm_cols % tn == 0 and tn % norm_group == 0
        n_norm_tiles = n_norm_cols // tn
        in_specs.append(pl.BlockSpec((1, tn), lambda i, j: (0, j)))
        args.append(gain.reshape(1, n))
    return pl.pallas_call(
        functools.partial(_mm_kernel, has_bias=bias is not None, gate_row=gate_row, norm_group=norm_group,
                          norm_div=norm_div, n_norm_tiles=n_norm_tiles),
        grid=(m // tm, n // tn), in_specs=in_specs,
        out_specs=pl.BlockSpec((tm, tn), lambda i, j: (i, j)),
        out_shape=jax.ShapeDtypeStruct((m, n), out_dtype),
        compiler_params=_cparams(("parallel", "parallel")), name="matmul",
    )(*args)


def _flash_kernel(q_ref, k_ref, v_ref, o_ref, m_sc, l_sc, acc_sc, *, scale, nk):
    ki = pl.program_id(3)

    @pl.when(ki == 0)
    def _():
        m_sc[...] = jnp.full(m_sc.shape, NEG, F32)
        l_sc[...] = jnp.zeros(l_sc.shape, F32)
        acc_sc[...] = jnp.zeros(acc_sc.shape, F32)

    q = q_ref[0].astype(BF)
    k = k_ref[0].astype(BF)
    v = v_ref[0].astype(BF)
    s = lax.dot_general(q, k, (((1,), (1,)), ((), ())), preferred_element_type=F32) * scale
    m_prev = m_sc[...]
    m_new = jnp.maximum(m_prev, jnp.max(s, axis=1, keepdims=True))
    alpha = jnp.exp(m_prev - m_new)
    p = jnp.exp(s - m_new)
    l_sc[...] = alpha * l_sc[...] + jnp.sum(p, axis=1, keepdims=True)
    acc_sc[...] = alpha * acc_sc[...] + jnp.dot(p.astype(BF), v, preferred_element_type=F32)
    m_sc[...] = m_new

    @pl.when(ki == nk - 1)
    def _():
        o_ref[0] = (acc_sc[...] / l_sc[...]).astype(o_ref.dtype)


def flash_attention(q, k, v, *, nb, nh, tq_len, tk_len, dk, dv, scale, q_map, k_map, v_map, tq, tk):
    nq, nk = tq_len // tq, tk_len // tk
    return pl.pallas_call(
        functools.partial(_flash_kernel, scale=scale, nk=nk),
        grid=(nb, nh, nq, nk),
        in_specs=[
            pl.BlockSpec((1, tq, dk), lambda b, h, qi, ki: (q_map(b, h)[0], qi, q_map(b, h)[1])),
            pl.BlockSpec((1, tk, dk), lambda b, h, qi, ki: (k_map(b, h)[0], ki, k_map(b, h)[1])),
            pl.BlockSpec((1, tk, dv), lambda b, h, qi, ki: (v_map(b, h)[0], ki, v_map(b, h)[1])),
        ],
        out_specs=pl.BlockSpec((1, tq, dv), lambda b, h, qi, ki: (b, qi, h)),
        out_shape=jax.ShapeDtypeStruct((nb, tq_len, nh * dv), BF),
        scratch_shapes=[pltpu.VMEM((tq, 1), F32), pltpu.VMEM((tq, 1), F32), pltpu.VMEM((tq, dv), F32)],
        compiler_params=_cparams(("parallel", "parallel", "parallel", "arbitrary")), name="flash_attention",
    )(q, k, v)


def _natten_kernel(q_ref, k_ref, v_ref, kc_ref, vc_ref, bias_ref, o_ref, *, scale):
    rb = pl.program_id(2)
    ws = jnp.clip(rb * NA_ROWS - WIN_R // 2, 0, GRID_ROWS - NA_KROWS)
    start = pl.multiple_of(ws * GRID_W, GRID_W)
    nkeys = NA_KROWS * GRID_W
    q = q_ref[0].astype(BF)
    kw = k_ref[0, pl.ds(start, nkeys), :].astype(BF)
    vw = v_ref[0, pl.ds(start, nkeys), :].astype(BF)
    nt = (((1,), (1,)), ((), ()))
    s_w = lax.dot_general(q, kw, nt, preferred_element_type=F32) * scale + bias_ref[0, 0]
    s_c = lax.dot_general(q, kc_ref[0].astype(BF), nt, preferred_element_type=F32) * scale
    m = jnp.maximum(jnp.max(s_w, axis=1, keepdims=True), jnp.max(s_c, axis=1, keepdims=True))
    p_w = jnp.exp(s_w - m)
    p_c = jnp.exp(s_c - m)
    l = jnp.sum(p_w, axis=1, keepdims=True) + jnp.sum(p_c, axis=1, keepdims=True)
    o = jnp.dot(p_w.astype(BF), vw, preferred_element_type=F32)
    o += jnp.dot(p_c.astype(BF), vc_ref[0].astype(BF), preferred_element_type=F32)
    o_ref[0] = (o / l).astype(o_ref.dtype)


def _natten_bias(rpb):
    tables = []
    for r0 in (0, 2 * NA_ROWS, GRID_ROWS - NA_ROWS):
        ws = int(np.clip(r0 - WIN_R // 2, 0, GRID_ROWS - NA_KROWS))
        r = r0 + np.arange(NA_ROWS)[:, None, None, None]
        cq = np.arange(GRID_W)[None, :, None, None]
        kr = ws + np.arange(NA_KROWS)[None, None, :, None]
        kc = np.arange(GRID_W)[None, None, None, :]
        rs = np.clip(r - WIN_R // 2, 0, GRID_ROWS - WIN_R)
        cs = np.clip(cq - WIN_C // 2, 0, GRID_W - WIN_C)
        ok = (kr >= rs) & (kr < rs + WIN_R) & (kc >= cs) & (kc < cs + WIN_C)
        dr = np.clip(kr - r + WIN_R - 1, 0, 2 * WIN_R - 2)
        dc = np.clip(kc - cq + WIN_C - 1, 0, 2 * WIN_C - 2)
        shape = (NA_ROWS, GRID_W, NA_KROWS, GRID_W)
        dr, dc, ok = (np.broadcast_to(t, shape).reshape(NA_ROWS * GRID_W, NA_KROWS * GRID_W) for t in (dr, dc, ok))
        tables.append(jnp.where(ok[None], rpb[:, dr, dc].astype(F32), NEG))
    return jnp.stack(tables, axis=1)


def natten(qkv, k_ctx, v_ctx, rpb):
    bias = _natten_bias(rpb)
    nrb = GRID_ROWS // NA_ROWS
    tq = NA_ROWS * GRID_W
    nkeys = NA_KROWS * GRID_W

    def case(rb):
        return jnp.where(rb == 0, 0, jnp.where(rb == nrb - 1, 2, 1))

    return pl.pallas_call(
        functools.partial(_natten_kernel, scale=DH_C ** -0.5),
        grid=(N_LAT_B, NH_C, nrb),
        in_specs=[
            pl.BlockSpec((1, tq, DH_C), lambda b, h, rb: (1 + b, rb, h)),
            pl.BlockSpec((1, LAT_T, DH_C), lambda b, h, rb: (1 + b, 0, NH_C + h)),
            pl.BlockSpec((1, LAT_T, DH_C), lambda b, h, rb: (1 + b, 0, 2 * NH_C + h)),
            pl.BlockSpec((1, PAST, DH_C), lambda b, h, rb: (b, 0, h)),
            pl.BlockSpec((1, PAST, DH_C), lambda b, h, rb: (b, 0, h)),
            pl.BlockSpec((1, 1, tq, nkeys), lambda b, h, rb: (h, case(rb), 0, 0)),
        ],
        out_specs=pl.BlockSpec((1, tq, DH_C), lambda b, h, rb: (b, rb, h)),
        out_shape=jax.ShapeDtypeStruct((N_LAT_B, LAT_T, NH_C * DH_C), BF),
        compiler_params=_cparams(("parallel", "parallel", "arbitrary")), name="natten",
    )(qkv, qkv, qkv, k_ctx, v_ctx, bias)


def _mlstm_kernel(q_ref, k_ref, v_ref, og_ref, gates_ref, c0_ref, nm0_ref, ghn_ref,
                  y_ref, cn_ref, nmn_ref, hf_sc, hb_sc, c_sc, *, nc):
    L = CHUNK
    row = lax.broadcasted_iota(jnp.int32, (L, L), 0)
    col = lax.broadcasted_iota(jnp.int32, (L, L), 1)
    eye = row == col
    nt = (((1,), (1,)), ((), ()))
    tn = (((0,), (0,)), ((), ()))

    def to_col(r):
        return jnp.sum(jnp.where(eye, jnp.broadcast_to(r, (L, L)), 0.0), axis=1, keepdims=True)

    def chunk(c, d, n, m, reverse):
        sl = pl.ds(pl.multiple_of(c * L, L), L)
        g = gates_ref[0, 0, c]
        ic_row = g[2 * d:2 * d + 1, :]
        lf_row = g[2 * d + 1:2 * d + 2, :]
        q = q_ref[0, sl, :]
        k = k_ref[0, sl, :] * (DQK_A ** -0.5)
        v = v_ref[0, sl, :].astype(BF)
        mask = (col >= row) if reverse else (col <= row)
        mask_t = (row >= col) if reverse else (row <= col)
        lf_b = jnp.broadcast_to(lf_row, (L, L))
        b_col = jnp.sum(jnp.where(mask, lf_b, 0.0), axis=1, keepdims=True)
        lf_col = jnp.sum(jnp.where(eye, lf_b, 0.0), axis=1, keepdims=True)
        b_row = jnp.sum(jnp.where(mask_t, lf_col, 0.0), axis=0, keepdims=True)
        ic_col = to_col(ic_row)
        dmat = jnp.where(mask, b_col - b_row + ic_row, NEG)
        inter = b_col + m
        m_t = jnp.maximum(inter, jnp.max(dmat, axis=1, keepdims=True))
        a = jnp.exp(inter - m_t)
        qb = q.astype(BF)
        s = lax.dot_general(qb, k.astype(BF), nt, preferred_element_type=F32) * jnp.exp(dmat - m_t)
        c_old = c_sc[d]
        num = a * jnp.dot(qb, c_old.astype(BF), preferred_element_type=F32)
        num += jnp.dot(s.astype(BF), v, preferred_element_type=F32)
        den = a * jnp.sum(q * n, axis=1, keepdims=True) + jnp.sum(s, axis=1, keepdims=True)
        h = num / jnp.maximum(jnp.abs(den), jnp.exp(-m_t))
        b_end = jnp.sum(lf_row, axis=1, keepdims=True)
        g_col = b_end - b_col + ic_col
        m_new = jnp.maximum(b_end + m, jnp.max(g_col, axis=0, keepdims=True))
        a_s = jnp.exp(b_end + m - m_new)
        kw = k * jnp.exp(g_col - m_new)
        c_sc[d] = a_s * c_old + lax.dot_general(kw.astype(BF), v, tn, preferred_element_type=F32)
        n_new = a_s * n + jnp.sum(kw, axis=0, keepdims=True)
        return h, sl, n_new, m_new

    c_sc[0] = c0_ref[0, 0, 0]
    c_sc[1] = c0_ref[0, 1, 0]
    init = (nm0_ref[0, 0, 0, 0:1, :], nm0_ref[0, 0, 0, 1:2, 0:1],
            nm0_ref[0, 1, 0, 0:1, :], nm0_ref[0, 1, 0, 1:2, 0:1])

    def body(i, carry):
        n_f, m_f, n_b, m_b = carry
        h, sl, n_f, m_f = chunk(i, 0, n_f, m_f, False)
        hf_sc[sl, :] = h
        h, sl, n_b, m_b = chunk(nc - 1 - i, 1, n_b, m_b, True)
        hb_sc[sl, :] = h
        return n_f, m_f, n_b, m_b

    n_f, m_f, n_b, m_b = lax.fori_loop(0, nc, body, init)

    cn_ref[0, 0, 0] = c_sc[0]
    cn_ref[0, 1, 0] = c_sc[1]
    for d, (n_d, m_d) in enumerate(((n_f, m_f), (n_b, m_b))):
        nmn_ref[0, d, 0] = jnp.concatenate(
            [n_d, jnp.broadcast_to(m_d, (1, DQK_A)), jnp.zeros((6, DQK_A), F32)], axis=0)

    rows = 256
    gain = ghn_ref[0]

    def out_body(i, _):
        sl = pl.ds(pl.multiple_of(i * rows, rows), rows)
        hs = hf_sc[sl, :] + hb_sc[sl, :]
        hn = hs * lax.rsqrt(jnp.mean(hs * hs, axis=-1, keepdims=True) + EPS) * gain
        gate = 1.0 / (1.0 + jnp.exp(-og_ref[0, sl, :]))
        y_ref[0, sl, :] = (gate * hn).astype(y_ref.dtype)
        return 0

    lax.fori_loop(0, (nc * L) // rows, out_body, 0)


def mlstm(proj, gates, c0, nm0, g_hn, *, nb, t_len, boff):
    nc = t_len // CHUNK
    kq = A_QK // DQK_A
    kv = 2 * A_QK // DV_A
    ko = (2 * A_QK + A_V) // DV_A
    return pl.pallas_call(
        functools.partial(_mlstm_kernel, nc=nc),
        grid=(nb, NH_A),
        in_specs=[
            pl.BlockSpec((1, t_len, DQK_A), lambda b, h: (boff + b, 0, h)),
            pl.BlockSpec((1, t_len, DQK_A), lambda b, h: (boff + b, 0, kq + h)),
            pl.BlockSpec((1, t_len, DV_A), lambda b, h: (boff + b, 0, kv + h)),
            pl.BlockSpec((1, t_len, DV_A), lambda b, h: (boff + b, 0, ko + h)),
            pl.BlockSpec((1, 1, nc, 4, CHUNK), lambda b, h: (b, h, 0, 0, 0)),
            pl.BlockSpec((1, 2, 1, DQK_A, DV_A), lambda b, h: (b, 0, h, 0, 0)),
            pl.BlockSpec((1, 2, 1, 8, DQK_A), lambda b, h: (b, 0, h, 0, 0)),
            pl.BlockSpec((1, 1, DV_A), lambda b, h: (h, 0, 0)),
        ],
        out_specs=[
            pl.BlockSpec((1, t_len, DV_A), lambda b, h: (b, 0, h)),
            pl.BlockSpec((1, 2, 1, DQK_A, DV_A), lambda b, h: (b, 0, h, 0, 0)),
            pl.BlockSpec((1, 2, 1, 8, DQK_A), lambda b, h: (b, 0, h, 0, 0)),
        ],
        out_shape=[
            jax.ShapeDtypeStruct((nb, t_len, A_V), BF),
            jax.ShapeDtypeStruct((nb, 2, NH_A, DQK_A, DV_A), F32),
            jax.ShapeDtypeStruct((nb, 2, NH_A, 8, DQK_A), F32),
        ],
        scratch_shapes=[pltpu.VMEM((t_len, DV_A), F32), pltpu.VMEM((t_len, DV_A), F32),
                        pltpu.VMEM((2, DQK_A, DV_A), F32)],
        compiler_params=_cparams(("parallel", "parallel")), name="mlstm",
    )(proj, proj, proj, proj, gates, c0, nm0, g_hn.reshape(NH_A, 1, DV_A))


def _moe_kernel(blk_e_ref, nused_ref, x_ref, w1g_ref, w1l_ref, b1g_ref, b1l_ref, w2_ref, b2_ref, o_ref, *, nf):
    b = pl.program_id(0)
    f = pl.program_id(1)
    active = b < nused_ref[0]

    @pl.when(active)
    def _():
        x = x_ref[...]
        hg = jnp.dot(x, w1g_ref[0], preferred_element_type=F32) + b1g_ref[0]
        hl = jnp.dot(x, w1l_ref[0], preferred_element_type=F32) + b1l_ref[0]
        glu = jnp.minimum(hg, SWIGLU_LIMIT)
        lin = jnp.clip(hl, -SWIGLU_LIMIT, SWIGLU_LIMIT)
        act = glu * (1.0 / (1.0 + jnp.exp(-SWIGLU_ALPHA * glu))) * (lin + 1.0)
        part = jnp.dot(act.astype(BF), w2_ref[0], preferred_element_type=F32)

        @pl.when(f == 0)
        def _():
            o_ref[...] = part + b2_ref[0]

        @pl.when(f > 0)
        def _():
            o_ref[...] += part

    @pl.when(jnp.logical_not(active) & (f == nf - 1))
    def _():
        o_ref[...] = jnp.zeros(o_ref.shape, o_ref.dtype)


def moe_experts(xb, blk_e, nused, w1g, w1l, b1g, b1l, w2, b2):
    nf = D_FF // MOE_TF
    d = D_MODEL

    def eidx(b, be):
        return be[b]

    def fidx(b, f, nu):
        return jnp.where(b < nu[0], f, nf - 1)

    grid_spec = pltpu.PrefetchScalarGridSpec(
        num_scalar_prefetch=2,
        grid=(MOE_NB, nf),
        in_specs=[
            pl.BlockSpec((MOE_BM, d), lambda b, f, be, nu: (jnp.minimum(b, nu[0] - 1), 0)),
            pl.BlockSpec((1, d, MOE_TF), lambda b, f, be, nu: (eidx(b, be), 0, fidx(b, f, nu))),
            pl.BlockSpec((1, d, MOE_TF), lambda b, f, be, nu: (eidx(b, be), 0, fidx(b, f, nu))),
            pl.BlockSpec((1, 1, MOE_TF), lambda b, f, be, nu: (eidx(b, be), 0, fidx(b, f, nu))),
            pl.BlockSpec((1, 1, MOE_TF), lambda b, f, be, nu: (eidx(b, be), 0, fidx(b, f, nu))),
            pl.BlockSpec((1, MOE_TF, d), lambda b, f, be, nu: (eidx(b, be), fidx(b, f, nu), 0)),
            pl.BlockSpec((1, 1, d), lambda b, f, be, nu: (eidx(b, be), 0, 0)),
        ],
        out_specs=pl.BlockSpec((MOE_BM, d), lambda b, f, be, nu: (b, 0)),
    )
    return pl.pallas_call(
        functools.partial(_moe_kernel, nf=nf),
        grid_spec=grid_spec,
        out_shape=jax.ShapeDtypeStruct((MOE_NB * MOE_BM, d), F32),
        compiler_params=_cparams(("arbitrary", "arbitrary")), name="moe_experts",
    )(blk_e, nused, xb, w1g, w1l, b1g, b1l, w2, b2)


def moe(hdn, logits, w1, b1, w2, b2):
    n = hdn.shape[0]
    nk = n * TOP_K
    top_v, top_i = lax.top_k(logits, TOP_K)
    gate = jax.nn.softmax(top_v, axis=-1)
    e = top_i.reshape(-1).astype(jnp.int32)
    onehot = (e[:, None] == jnp.arange(N_EXP, dtype=jnp.int32)[None, :]).astype(jnp.int32)
    csum = jnp.cumsum(onehot, axis=0)
    rank = jnp.take_along_axis(csum, e[:, None], axis=1)[:, 0] - 1
    counts = csum[-1]
    padded = (counts + MOE_BM - 1) // MOE_BM * MOE_BM
    pad_end = jnp.cumsum(padded)
    pad_start = pad_end - padded
    dest = pad_start[e] + rank
    tok = jnp.arange(nk, dtype=jnp.int32) // TOP_K
    slot_tok = jnp.zeros((MOE_NB * MOE_BM,), jnp.int32).at[dest].set(tok)
    blk_e = jnp.minimum(
        jnp.searchsorted(pad_end, jnp.arange(MOE_NB, dtype=jnp.int32) * MOE_BM, side="right"), N_EXP - 1
    ).astype(jnp.int32)
    nused = (pad_end[-1] // MOE_BM).astype(jnp.int32).reshape(1)
    xb = jnp.take(hdn, slot_tok, axis=0)
    w1b = w1.astype(BF)
    yb = moe_experts(xb, blk_e, nused, w1b[:, :, 0::2], w1b[:, :, 1::2],
                     b1[:, None, 0::2], b1[:, None, 1::2], w2.astype(BF), b2[:, None, :])
    y = jnp.take(yb, dest, axis=0).reshape(n, TOP_K, -1)
    return jnp.sum(y * gate[:, :, None], axis=1)


def _rms(x, g):
    return x * lax.rsqrt(jnp.mean(x * x, axis=-1, keepdims=True) + EPS) * g


def _rotary_lat(x):
    half, quarter = ROPE // 2, ROPE // 4
    inv = ROPE_BASE ** (-jnp.arange(quarter, dtype=F32) * 2.0 / half)
    t = jnp.arange(LAT_T)
    extra = (1,) * (x.ndim - 3)

    def rot(seg, pos):
        ang = (pos.astype(F32)[:, None] * inv).reshape((LAT_T,) + extra + (quarter,))
        cos, sin = jnp.cos(ang), jnp.sin(ang)
        s1, s2 = seg[..., :quarter], seg[..., quarter:]
        return jnp.concatenate([s1 * cos - s2 * sin, s1 * sin + s2 * cos], axis=-1)

    return jnp.concatenate([rot(x[..., :half], t // GRID_W), rot(x[..., half:], t % GRID_W)], axis=-1)


def _pad_cols(w, width):
    return jnp.pad(w, ((0, 0), (0, width - w.shape[1])))


def _adaln(cond, ada_w, ada_b):
    a = jnp.pad(jax.nn.silu(cond), ((0, 8 - cond.shape[0]), (0, 0)))
    mod = matmul(a, ada_w, bias=ada_b, tm=8, tn=1024)
    return mod[:cond.shape[0]].reshape(cond.shape[0], 6, D_MODEL)


def _layer0_mixer(x, mod, g_mix, w_in, b_gate, g_qa, w_qb, g_kva, w_kvb, g_qn, g_kn, g_hn, w_out, cache):
    st_c, st_n, st_m, c_ckv, c_kpe = cache
    o_gate = 2 * A_QK + 2 * A_V
    o_cq = o_gate + N_GATE
    w_gate = _pad_cols(w_in[:, o_gate:o_cq], AUX_W)
    hdn, gates = norm_mod(x, g_mix, mod, 0, aux_w=w_gate, aux_b=jnp.pad(b_gate, (0, AUX_W - N_GATE)))
    proj = matmul(hdn, w_in[:, :o_gate].astype(BF))
    w_b = _pad_cols(w_in[:, o_cq:], MLA_IN_W).astype(BF)
    proj_b = matmul(hdn, w_b, tn=MLA_IN_W)

    gt = GATE_CAP * jnp.tanh(gates[:, :N_GATE] / GATE_CAP)
    gt = gt.reshape(N_TOK, 4, NH_A)
    gt = jnp.stack([gt[:, 0], jax.nn.log_sigmoid(gt[:, 1]), gt[:, 2], jax.nn.log_sigmoid(gt[:, 3])], axis=1)

    def gate_layout(g, nb, t_len):
        g = g.reshape(nb, t_len // CHUNK, CHUNK, 4, NH_A)
        return g.transpose(0, 4, 1, 3, 2)

    zc = jnp.zeros((N_CTX_B, 2, NH_A, DQK_A, DV_A), F32)
    znm = jnp.zeros((N_CTX_B, 2, NH_A, 8, DQK_A), F32)
    y_ctx, new_c, new_nm = mlstm(proj.reshape(N_TOK // CTX_T, CTX_T, -1), gate_layout(gt[:N_CTX], N_CTX_B, CTX_T),
                                 zc, znm, g_hn, nb=N_CTX_B, t_len=CTX_T, boff=0)
    nm_lat = jnp.concatenate([st_n[:, :, :, None, :],
                              jnp.broadcast_to(st_m[:, :, :, None, None], (N_LAT_B, 2, NH_A, 1, DQK_A)),
                              jnp.zeros((N_LAT_B, 2, NH_A, 6, DQK_A), F32)], axis=3)
    y_lat, _, _ = mlstm(proj.reshape(N_TOK // LAT_T, LAT_T, -1), gate_layout(gt[N_CTX:], N_LAT_B, LAT_T),
                        st_c, nm_lat, g_hn, nb=N_LAT_B, t_len=LAT_T, boff=1)
    y_a = jnp.concatenate([y_ctx.reshape(N_CTX, A_V), y_lat.reshape(N_LAT, A_V)], axis=0)
    new_n = new_nm[:, :, :, 0, :]
    new_m = new_nm[:, :, :, 1, 0]

    cq = proj_b[:, :Q_LORA]
    ckv = proj_b[:, Q_LORA:Q_LORA + KV_LORA]
    kpe = proj_b[:, Q_LORA + KV_LORA:Q_LORA + KV_LORA + ROPE]
    wq = w_qb.reshape(Q_LORA, NH_B, QK_B)
    wq = jnp.pad(wq, ((0, 0), (0, 0), (0, QK_B_PAD - QK_B))).reshape(Q_LORA, NH_B * QK_B_PAD).astype(BF)
    gq = jnp.tile(jnp.pad(g_qn, (0, QK_B_PAD - QK_B)), NH_B)
    q = matmul(_rms(cq, g_qa), wq, gain=gq, norm_group=QK_B_PAD, norm_div=float(QK_B), tn=1024)
    ckv_n = _rms(ckv, g_kva)
    kv_in = jnp.concatenate([ckv_n, c_ckv.reshape(N_LAT_B * PAST, KV_LORA)], axis=0)
    kv = matmul(kv_in, w_kvb.astype(BF), tn=1024)
    kv = kv.reshape(-1, NH_B, NOPE + VH)
    kpe_all = jnp.concatenate([kpe, c_kpe.reshape(N_LAT_B * PAST, ROPE)], axis=0)
    k = jnp.concatenate([kv[:, :, :NOPE], jnp.broadcast_to(kpe_all[:, None, :], (kv.shape[0], NH_B, ROPE))], axis=-1)
    k = _rms(k, g_kn)
    v = kv[:, :, NOPE:].astype(BF)
    q = q.reshape(N_TOK, NH_B, QK_B_PAD)

    def rot_tail(t):
        t4 = t.reshape(N_LAT_B, LAT_T, NH_B, t.shape[-1])
        r = _rotary_lat(t4[..., NOPE:QK_B])
        return jnp.concatenate([t4[..., :NOPE], r, t4[..., QK_B:]], axis=-1).reshape(t.shape)

    q = jnp.concatenate([q[:N_CTX], rot_tail(q[N_CTX:])], axis=0).astype(BF)
    k_new = jnp.concatenate([k[:N_CTX], rot_tail(k[N_CTX:N_TOK])], axis=0)
    kpad = ((0, 0), (0, 0), (0, QK_B_PAD - QK_B))
    k_new = jnp.pad(k_new, kpad).astype(BF).reshape(N_TOK, NH_B * QK_B_PAD)
    k_old = jnp.pad(k[N_TOK:], kpad).astype(BF).reshape(N_LAT_B, PAST, NH_B * QK_B_PAD)
    q = q.reshape(N_TOK, NH_B * QK_B_PAD)
    v_new = v[:N_TOK].reshape(N_TOK, NH_B * VH)
    v_old = v[N_TOK:].reshape(N_LAT_B, PAST, NH_B * VH)
    scale = QK_B ** -0.5
    ident = lambda b, h: (b, h)
    y_b_ctx = flash_attention(
        q[:N_CTX].reshape(N_CTX_B, CTX_T, -1), k_new[:N_CTX].reshape(N_CTX_B, CTX_T, -1),
        v_new[:N_CTX].reshape(N_CTX_B, CTX_T, -1), nb=N_CTX_B, nh=NH_B, tq_len=CTX_T, tk_len=CTX_T,
        dk=QK_B_PAD, dv=VH, scale=scale, q_map=ident, k_map=ident, v_map=ident, tq=CTX_T, tk=CTX_T)
    k_lat = jnp.concatenate([k_new[N_CTX:].reshape(N_LAT_B, LAT_T, -1), k_old], axis=1)
    v_lat = jnp.concatenate([v_new[N_CTX:].reshape(N_LAT_B, LAT_T, -1), v_old], axis=1)
    y_b_lat = flash_attention(
        q[N_CTX:].reshape(N_LAT_B, LAT_T, -1), k_lat, v_lat, nb=N_LAT_B, nh=NH_B, tq_len=LAT_T,
        tk_len=LAT_T + PAST, dk=QK_B_PAD, dv=VH, scale=scale, q_map=ident, k_map=ident, v_map=ident,
        tq=512, tk=512)
    y_b = jnp.concatenate([y_b_ctx.reshape(N_CTX, -1), y_b_lat.reshape(N_LAT, -1)], axis=0)

    y = jnp.concatenate([y_a, y_b], axis=1)
    x = matmul(y, w_out.astype(BF), resid=x, mod=mod, gate_row=2)
    new = (new_c, new_n, new_m, ckv_n[:N_CTX].reshape(N_CTX_B, CTX_T, KV_LORA),
           kpe[:N_CTX].reshape(N_CTX_B, CTX_T, ROPE))
    return x, new


def _layer1_mixer(x, mod, g_mix, w_qkv, g_qn, g_kn, rpb, w_out, cache):
    c_k, c_v = cache
    hd = NH_C * DH_C
    hdn = norm_mod(x, g_mix, mod, 0)
    gain = jnp.concatenate([jnp.tile(g_qn, NH_C), jnp.tile(g_kn, NH_C), jnp.ones((hd,), F32)])
    qkv = matmul(hdn, w_qkv.astype(BF), gain=gain, norm_group=DH_C, norm_div=float(DH_C), n_norm_cols=2 * hd)
    new_k = qkv[:N_CTX, hd:2 * hd].reshape(N_CTX_B, CTX_T, NH_C, DH_C)
    new_v = qkv[:N_CTX, 2 * hd:].reshape(N_CTX_B, CTX_T, NH_C, DH_C)
    qkv_c = qkv.reshape(N_TOK // CTX_T, CTX_T, 3 * hd)
    o_ctx = flash_attention(
        qkv_c, qkv_c, qkv_c, nb=N_CTX_B, nh=NH_C, tq_len=CTX_T, tk_len=CTX_T, dk=DH_C, dv=DH_C,
        scale=DH_C ** -0.5, q_map=lambda b, h: (b, h), k_map=lambda b, h: (b, NH_C + h),
        v_map=lambda b, h: (b, 2 * NH_C + h), tq=CTX_T, tk=CTX_T)
    o_lat = natten(qkv.reshape(N_TOK // LAT_T, LAT_T, 3 * hd), c_k.reshape(N_LAT_B, PAST, hd),
                   c_v.reshape(N_LAT_B, PAST, hd), rpb)
    o = jnp.concatenate([o_ctx.reshape(N_CTX, hd), o_lat.reshape(N_LAT, hd)], axis=0)
    x = matmul(o, w_out.astype(BF), resid=x, mod=mod, gate_row=2)
    return x, (new_k, new_v)


def _moe_sublayer(x, mod, g_moe, w_r, b_r, w1, b1, w2, b2):
    hdn, logits = norm_mod(x, g_moe, mod, 1, aux_w=_pad_cols(w_r, AUX_W), aux_b=jnp.pad(b_r, (0, AUX_W - N_EXP)))
    y = moe(hdn, logits[:, :N_EXP], w1, b1, w2, b2)
    ga = jnp.repeat(mod[:, 5, :], N_CTX, axis=0, total_repeat_length=N_TOK)
    return x + ga * y


def kernel(x_prompt, x_sample, state_l0_mlstm_C, state_l0_mlstm_n, state_l0_mlstm_m, cache_l0_mla_ckv, cache_l0_mla_kpe, cache_l1_na_k, cache_l1_na_v, c, c_ctx, l0_g_mix, l0_g_moe, l0_ada_w, l0_ada_b, l0_w_in, l0_b_gate, l0_g_qa, l0_w_qb, l0_g_kva, l0_w_kvb, l0_g_qn, l0_g_kn, l0_g_hn, l0_w_out, l0_w_router, l0_b_router, l0_w1, l0_b1, l0_w2, l0_b2, l1_g_mix, l1_g_moe, l1_ada_w, l1_ada_b, l1_w_qkv, l1_g_qn, l1_g_kn, l1_rpb, l1_w_out, l1_w_router, l1_b_router, l1_w1, l1_b1, l1_w2, l1_b2):
    x = jnp.concatenate([x_prompt.reshape(N_CTX, D_MODEL), x_sample.reshape(N_LAT, D_MODEL)], axis=0)
    cond = jnp.concatenate([c_ctx[None, :], c], axis=0)

    mod0 = _adaln(cond, l0_ada_w, l0_ada_b)
    x, (new_c, new_n, new_m, new_ckv, new_kpe) = _layer0_mixer(
        x, mod0, l0_g_mix, l0_w_in, l0_b_gate, l0_g_qa, l0_w_qb, l0_g_kva, l0_w_kvb, l0_g_qn, l0_g_kn, l0_g_hn,
        l0_w_out, (state_l0_mlstm_C, state_l0_mlstm_n, state_l0_mlstm_m, cache_l0_mla_ckv, cache_l0_mla_kpe))
    x = _moe_sublayer(x, mod0, l0_g_moe, l0_w_router, l0_b_router, l0_w1, l0_b1, l0_w2, l0_b2)

    mod1 = _adaln(cond, l1_ada_w, l1_ada_b)
    x, (new_k, new_v) = _layer1_mixer(x, mod1, l1_g_mix, l1_w_qkv, l1_g_qn, l1_g_kn, l1_rpb, l1_w_out,
                                      (cache_l1_na_k, cache_l1_na_v))
    x = _moe_sublayer(x, mod1, l1_g_moe, l1_w_router, l1_b_router, l1_w1, l1_b1, l1_w2, l1_b2)

    y_prompt = x[:N_CTX].reshape(N_CTX_B, CTX_T, D_MODEL)
    y_sample = x[N_CTX:].reshape(N_LAT_B, LAT_T, D_MODEL)
    return (y_prompt, y_sample, new_c, new_n, new_m, new_ckv, new_kpe, new_k, new_v)
```

```python
import functools

import numpy as np
import jax
import jax.numpy as jnp
from jax import lax
from jax.experimental import pallas as pl
from jax.experimental.pallas import tpu as pltpu

BF = jnp.bfloat16
F32 = jnp.float32

D_MODEL = 2048
N_CTX_B, CTX_T = 16, 256
N_LAT_B, LAT_T = 4, 4096
N_CTX = N_CTX_B * CTX_T
N_LAT = N_LAT_B * LAT_T
N_TOK = N_CTX + N_LAT
PAST = 512
GRID_W = 64
GRID_ROWS = LAT_T // GRID_W
EPS = 1e-6
NEG = -1e30

NH_A, DQK_A, DV_A = 4, 128, 256
A_QK, A_V = NH_A * DQK_A, NH_A * DV_A
N_GATE = 4 * NH_A
CHUNK = 64
GATE_CAP = 15.0
NH_B, Q_LORA, KV_LORA, NOPE, ROPE, VH = 8, 512, 512, 128, 64, 128
QK_B = NOPE + ROPE
QK_B_PAD = 256
ROPE_BASE = 10000.0
NH_C, DH_C = 16, 128
WIN_R, WIN_C = 8, 16
NA_ROWS = 4
NA_KROWS = 12
NA_HEADS = 2
LOG2E = 1.4426950408889634
N_EXP, TOP_K, D_FF = 32, 4, 2048
SWIGLU_ALPHA, SWIGLU_LIMIT = 1.702, 7.0
MOE_BM = 512
MOE_TF = 1024
MOE_GRP = 256
MOE_NB = (N_TOK * TOP_K + N_EXP * (MOE_BM - 1) + MOE_BM - 1) // MOE_BM
AUX_W = 128
MLA_IN_W = 1280

VMEM_LIMIT = 56 * 1024 * 1024


def _cparams(sem):
    return pltpu.CompilerParams(dimension_semantics=sem, vmem_limit_bytes=VMEM_LIMIT)


def _split_bf16(x):
    hi = x.astype(BF)
    lo = (x - hi.astype(F32)).astype(BF)
    return hi, lo


def _norm_mod_kernel(*refs, which, has_aux):
    if has_aux:
        x_ref, g_ref, mod_ref, w_ref, b_ref, o_ref, aux_ref = refs
    else:
        x_ref, g_ref, mod_ref, o_ref = refs
    x = x_ref[...]
    ms = jnp.mean(x * x, axis=-1, keepdims=True)
    y = x * lax.rsqrt(ms + EPS) * g_ref[...]
    sh = mod_ref[0, 3 * which:3 * which + 1, :]
    sc = mod_ref[0, 3 * which + 1:3 * which + 2, :]
    h = y * (1.0 + sc) + sh
    o_ref[...] = h.astype(BF)
    if has_aux:
        h_hi, h_lo = _split_bf16(h)
        w_hi, w_lo = _split_bf16(w_ref[...])
        acc = jnp.dot(h_hi, w_hi, preferred_element_type=F32)
        acc += jnp.dot(h_hi, w_lo, preferred_element_type=F32)
        acc += jnp.dot(h_lo, w_hi, preferred_element_type=F32)
        aux_ref[...] = acc + b_ref[...]


def norm_mod(x, g, mod, which, aux_w=None, aux_b=None, tm=512):
    n, d = x.shape
    has_aux = aux_w is not None
    in_specs = [
        pl.BlockSpec((tm, d), lambda i: (i, 0)),
        pl.BlockSpec((1, d), lambda i: (0, 0)),
        pl.BlockSpec((1, 6, d), lambda i: ((i * tm) // N_CTX, 0, 0)),
    ]
    args = [x, g.reshape(1, d), mod]
    out_shape = [jax.ShapeDtypeStruct((n, d), BF)]
    out_specs = [pl.BlockSpec((tm, d), lambda i: (i, 0))]
    if has_aux:
        in_specs += [pl.BlockSpec((d, AUX_W), lambda i: (0, 0)), pl.BlockSpec((1, AUX_W), lambda i: (0, 0))]
        args += [aux_w, aux_b.reshape(1, AUX_W)]
        out_shape.append(jax.ShapeDtypeStruct((n, AUX_W), F32))
        out_specs.append(pl.BlockSpec((tm, AUX_W), lambda i: (i, 0)))
    res = pl.pallas_call(
        functools.partial(_norm_mod_kernel, which=which, has_aux=has_aux),
        grid=(n // tm,), in_specs=in_specs, out_specs=out_specs, out_shape=out_shape,
        compiler_params=_cparams(("parallel",)), name="norm_mod",
    )(*args)
    return res if has_aux else res[0]


def _group_rmsnorm(acc, gain, group, div):
    parts = []
    for c in range(acc.shape[1] // group):
        blk = acc[:, c * group:(c + 1) * group]
        ms = jnp.sum(blk * blk, axis=-1, keepdims=True) * (1.0 / div)
        parts.append(blk * lax.rsqrt(ms + EPS) * gain[:, c * group:(c + 1) * group])
    return parts[0] if len(parts) == 1 else jnp.concatenate(parts, axis=-1)


def _mm_kernel(*refs, has_bias, gate_row, norm_group, norm_div, n_norm_tiles):
    a_ref, w_ref = refs[0], refs[1]
    pos = 2
    if has_bias:
        b_ref = refs[pos]; pos += 1
    if gate_row is not None:
        x_ref, mod_ref = refs[pos], refs[pos + 1]; pos += 2
    if norm_group:
        gain_ref = refs[pos]; pos += 1
    o_ref = refs[pos]

    acc = jnp.dot(a_ref[...].astype(BF), w_ref[...].astype(BF), preferred_element_type=F32)
    if has_bias:
        acc = acc + b_ref[...]
    if gate_row is not None:
        acc = x_ref[...] + mod_ref[0, gate_row:gate_row + 1, :] * acc
    if norm_group:
        j = pl.program_id(1)

        @pl.when(j < n_norm_tiles)
        def _():
            o_ref[...] = _group_rmsnorm(acc, gain_ref[...], norm_group, norm_div).astype(o_ref.dtype)

        @pl.when(j >= n_norm_tiles)
        def _():
            o_ref[...] = acc.astype(o_ref.dtype)
    else:
        o_ref[...] = acc.astype(o_ref.dtype)


def matmul(a, w, *, bias=None, resid=None, mod=None, gate_row=None, gain=None, norm_group=0,
           norm_div=1.0, n_norm_cols=None, out_dtype=F32, tm=1024, tn=512):
    m, k = a.shape
    n = w.shape[1]
    tm = min(tm, m)
    tn = min(tn, n)
    assert m % tm == 0 and n % tn == 0
    in_specs = [pl.BlockSpec((tm, k), lambda i, j: (i, 0)), pl.BlockSpec((k, tn), lambda i, j: (0, j))]
    args = [a, w]
    if bias is not None:
        in_specs.append(pl.BlockSpec((1, tn), lambda i, j: (0, j)))
        args.append(bias.reshape(1, n))
    if gate_row is not None:
        in_specs.append(pl.BlockSpec((tm, tn), lambda i, j: (i, j)))
        in_specs.append(pl.BlockSpec((1, 6, tn), lambda i, j: ((i * tm) // N_CTX, 0, j)))
        args += [resid, mod]
    n_norm_tiles = 0
    if norm_group:
        n_norm_cols = n if n_norm_cols is None else n_norm_cols
        assert n_norm_cols % tn == 0 and tn % norm_group == 0
        n_norm_tiles = n_norm_cols // tn
        in_specs.append(pl.BlockSpec((1, tn), lambda i, j: (0, j)))
        args.append(gain.reshape(1, n))
    return pl.pallas_call(
        functools.partial(_mm_kernel, has_bias=bias is not None, gate_row=gate_row, norm_group=norm_group,
                          norm_div=norm_div, n_norm_tiles=n_norm_tiles),
        grid=(m // tm, n // tn), in_specs=in_specs,
        out_specs=pl.BlockSpec((tm, tn), lambda i, j: (i, j)),
        out_shape=jax.ShapeDtypeStruct((m, n), out_dtype),
        compiler_params=_cparams(("parallel", "parallel")), name="matmul",
    )(*args)


ATTN_CHUNK = 512


def _attn_kernel(*refs, q_scale, n_src):
    q_ref, o_ref = refs[0], refs[-1]
    q = q_ref[0]
    if q_scale is not None:
        q = q * (q_scale * LOG2E)
    q = q.astype(BF)
    nt = (((1,), (1,)), ((), ()))
    scores, values = [], []
    for i in range(n_src):
        k_ref, v_ref = refs[1 + 2 * i], refs[2 + 2 * i]
        tk = k_ref.shape[1]
        ch = min(ATTN_CHUNK, tk)
        for c in range(tk // ch):
            k = k_ref[0, c * ch:(c + 1) * ch, :].astype(BF)
            scores.append(lax.dot_general(q, k, nt, preferred_element_type=F32))
            values.append((v_ref, c * ch, ch))
    m = functools.reduce(jnp.maximum, [jnp.max(s, axis=1, keepdims=True) for s in scores])
    l, acc = None, None
    for s, (v_ref, start, ch) in zip(scores, values):
        p = jnp.exp2(s - m)
        ps = jnp.sum(p, axis=1, keepdims=True)
        pv = jnp.dot(p.astype(BF), v_ref[0, start:start + ch, :].astype(BF), preferred_element_type=F32)
        l = ps if l is None else l + ps
        acc = pv if acc is None else acc + pv
    o_ref[0] = (acc / l).astype(o_ref.dtype)


def attention(q, q_map, kv_sources, *, nb, nh, tq_len, dk, dv, tq, q_scale):
    in_specs = [pl.BlockSpec((1, tq, dk), lambda b, h, qi: (q_map(b, h)[0], qi, q_map(b, h)[1]))]
    args = [q]
    for k, k_map, v, v_map, n_keys in kv_sources:
        in_specs.append(pl.BlockSpec((1, n_keys, dk), lambda b, h, qi, f=k_map: (f(b, h)[0], 0, f(b, h)[1])))
        in_specs.append(pl.BlockSpec((1, n_keys, dv), lambda b, h, qi, f=v_map: (f(b, h)[0], 0, f(b, h)[1])))
        args += [k, v]
    return pl.pallas_call(
        functools.partial(_attn_kernel, q_scale=q_scale, n_src=len(kv_sources)),
        grid=(nb, nh, tq_len // tq), in_specs=in_specs,
        out_specs=pl.BlockSpec((1, tq, dv), lambda b, h, qi: (b, qi, h)),
        out_shape=jax.ShapeDtypeStruct((nb, tq_len, nh * dv), BF),
        compiler_params=_cparams(("parallel", "parallel", "arbitrary")), name="attention",
    )(*args)


def _rotate(x, cos, sin_a, sin_b):
    quarter = ROPE // 4
    return x * cos + pltpu.roll(x, 128 - quarter, 1) * sin_a + pltpu.roll(x, quarter, 1) * sin_b


def _mla_q_kernel(cq_ref, gqa_ref, wq_ref, gq_ref, cos_ref, sa_ref, sb_ref, o_ref, *, out_scale):
    cq = cq_ref[...]
    cqn = cq * lax.rsqrt(jnp.mean(cq * cq, axis=-1, keepdims=True) + EPS) * gqa_ref[...]
    q = jnp.dot(cqn.astype(BF), wq_ref[...], preferred_element_type=F32)
    cos, sa, sb = cos_ref[...], sa_ref[...], sb_ref[...]
    for h in range(NH_B):
        blk = q[:, h * QK_B_PAD:(h + 1) * QK_B_PAD]
        r = lax.rsqrt(jnp.sum(blk * blk, axis=-1, keepdims=True) * (1.0 / QK_B) + EPS)
        y = blk * r * gq_ref[:, h * QK_B_PAD:(h + 1) * QK_B_PAD]
        o_ref[:, h * QK_B_PAD:h * QK_B_PAD + NOPE] = (y[:, :NOPE] * out_scale).astype(o_ref.dtype)
        o_ref[:, h * QK_B_PAD + NOPE:(h + 1) * QK_B_PAD] = (
            _rotate(y[:, NOPE:], cos, sa, sb) * out_scale).astype(o_ref.dtype)


def mla_q(proj_b, g_qa, wq, gq, tables, tm=512):
    n = proj_b.shape[0]
    width = NH_B * QK_B_PAD
    tab = pl.BlockSpec((tm, 128), lambda i: (i, 0))
    return pl.pallas_call(
        functools.partial(_mla_q_kernel, out_scale=QK_B ** -0.5 * LOG2E),
        grid=(n // tm,),
        in_specs=[pl.BlockSpec((tm, Q_LORA), lambda i: (i, 0)), pl.BlockSpec((1, Q_LORA), lambda i: (0, 0)),
                  pl.BlockSpec((Q_LORA, width), lambda i: (0, 0)), pl.BlockSpec((1, width), lambda i: (0, 0)),
                  tab, tab, tab],
        out_specs=pl.BlockSpec((tm, width), lambda i: (i, 0)),
        out_shape=jax.ShapeDtypeStruct((n, width), BF),
        compiler_params=_cparams(("parallel",)), name="mla_q",
    )(proj_b, g_qa.reshape(1, Q_LORA), wq, gq.reshape(1, width), *tables)


def _mla_kv_kernel(*refs, normalize, rotate):
    if rotate:
        ckv_ref, kpe_ref, gkva_ref, w_ref, gkn_ref, gkr_ref, cos_ref, sa_ref, sb_ref = refs[:9]
        outs = refs[9:]
    else:
        ckv_ref, kpe_ref, gkva_ref, w_ref, gkn_ref, gkr_ref = refs[:6]
        outs = refs[6:]
    k_ref, v_ref = outs[0], outs[1]
    c = ckv_ref[...]
    if normalize:
        c = c * lax.rsqrt(jnp.mean(c * c, axis=-1, keepdims=True) + EPS) * gkva_ref[...]
        outs[2][...] = c
    kv = jnp.dot(c.astype(BF), w_ref[...], preferred_element_type=F32)
    kpe = kpe_ref[...]
    ss_pe = jnp.sum(kpe * kpe, axis=-1, keepdims=True)
    for h in range(NH_B):
        kn = kv[:, h * (NOPE + VH):h * (NOPE + VH) + NOPE]
        r = lax.rsqrt((jnp.sum(kn * kn, axis=-1, keepdims=True) + ss_pe) * (1.0 / QK_B) + EPS)
        kr = kpe * r * gkr_ref[...]
        if rotate:
            kr = _rotate(kr, cos_ref[...], sa_ref[...], sb_ref[...])
        k_ref[:, h * QK_B_PAD:h * QK_B_PAD + NOPE] = (kn * r * gkn_ref[...]).astype(k_ref.dtype)
        k_ref[:, h * QK_B_PAD + NOPE:(h + 1) * QK_B_PAD] = kr.astype(k_ref.dtype)
        v_ref[:, h * VH:(h + 1) * VH] = kv[:, h * (NOPE + VH) + NOPE:(h + 1) * (NOPE + VH)].astype(v_ref.dtype)


def mla_kv(ckv_src, ckv_blk, kpe_src, kpe_blk, g_kva, w_kvb, g_kn, tables, *, normalize, tm=512):
    n = ckv_src.shape[0]
    rotate = tables is not None
    kw, vw = NH_B * QK_B_PAD, NH_B * VH
    in_specs = [pl.BlockSpec((tm, KV_LORA), lambda i: (i, ckv_blk)), pl.BlockSpec((tm, 128), lambda i: (i, kpe_blk)),
                pl.BlockSpec((1, KV_LORA), lambda i: (0, 0)), pl.BlockSpec((KV_LORA, NH_B * (NOPE + VH)), lambda i: (0, 0)),
                pl.BlockSpec((1, NOPE), lambda i: (0, 0)), pl.BlockSpec((1, 128), lambda i: (0, 0))]
    args = [ckv_src, kpe_src, g_kva.reshape(1, KV_LORA), w_kvb, g_kn[:NOPE].reshape(1, NOPE),
            jnp.pad(g_kn[NOPE:], (0, 128 - ROPE)).reshape(1, 128)]
    if rotate:
        in_specs += [pl.BlockSpec((tm, 128), lambda i: (i, 0))] * 3
        args += list(tables)
    out_shape = [jax.ShapeDtypeStruct((n, kw), BF), jax.ShapeDtypeStruct((n, vw), BF)]
    out_specs = [pl.BlockSpec((tm, kw), lambda i: (i, 0)), pl.BlockSpec((tm, vw), lambda i: (i, 0))]
    if normalize:
        out_shape.append(jax.ShapeDtypeStruct((n, KV_LORA), F32))
        out_specs.append(pl.BlockSpec((tm, KV_LORA), lambda i: (i, 0)))
    return pl.pallas_call(
        functools.partial(_mla_kv_kernel, normalize=normalize, rotate=rotate),
        grid=(n // tm,), in_specs=in_specs, out_specs=out_specs, out_shape=out_shape,
        compiler_params=_cparams(("parallel",)), name="mla_kv",
    )(*args)


def _rotary_tables():
    half, quarter = ROPE // 2, ROPE // 4
    inv = ROPE_BASE ** (-jnp.arange(quarter, dtype=F32) * 2.0 / half)
    t = jnp.arange(LAT_T)
    lane = np.arange(128)
    is_s2 = ((lane % half) >= quarter) & (lane < ROPE)
    is_s1 = ((lane % half) < quarter) & (lane < ROPE)
    pos = jnp.where((lane < half)[None, :], (t // GRID_W)[:, None], (t % GRID_W)[:, None]).astype(F32)
    ang = pos * inv[lane % quarter][None, :]
    live = (lane < ROPE)[None, :]
    cos = jnp.where(live, jnp.cos(ang), 1.0)
    sin = jnp.sin(ang)
    sin_a = jnp.where(is_s1[None, :], -sin, 0.0)
    sin_b = jnp.where(is_s2[None, :], sin, 0.0)

    def full(tab, fill):
        return jnp.concatenate([jnp.full((N_CTX, 128), fill, F32), jnp.tile(tab, (N_LAT_B, 1))], axis=0)

    return full(cos, 1.0), full(sin_a, 0.0), full(sin_b, 0.0)


def _natten_kernel(q_ref, k_ref, v_ref, kc_ref, vc_ref, bias_ref, o_ref, *, scale):
    rb = pl.program_id(2)
    ws = jnp.clip(rb * NA_ROWS - WIN_R // 2, 0, GRID_ROWS - NA_KROWS)
    start = pl.multiple_of(ws * GRID_W, GRID_W)
    nkeys = NA_KROWS * GRID_W
    nt = (((1,), (1,)), ((), ()))
    for h in range(NA_HEADS):
        cs = slice(h * DH_C, (h + 1) * DH_C)
        q = (q_ref[0, :, cs] * (scale * LOG2E)).astype(BF)
        kw = k_ref[0, pl.ds(start, nkeys), cs].astype(BF)
        vw = v_ref[0, pl.ds(start, nkeys), cs].astype(BF)
        s_w = lax.dot_general(q, kw, nt, preferred_element_type=F32) + bias_ref[h, 0]
        s_c = lax.dot_general(q, kc_ref[0, :, cs].astype(BF), nt, preferred_element_type=F32)
        m = jnp.maximum(jnp.max(s_w, axis=1, keepdims=True), jnp.max(s_c, axis=1, keepdims=True))
        p_w = jnp.exp2(s_w - m)
        p_c = jnp.exp2(s_c - m)
        l = jnp.sum(p_w, axis=1, keepdims=True) + jnp.sum(p_c, axis=1, keepdims=True)
        o = jnp.dot(p_w.astype(BF), vw, preferred_element_type=F32)
        o += jnp.dot(p_c.astype(BF), vc_ref[0, :, cs].astype(BF), preferred_element_type=F32)
        o_ref[0, :, cs] = (o / l).astype(o_ref.dtype)


def _natten_bias(rpb):
    tables = []
    for r0 in (0, 2 * NA_ROWS, GRID_ROWS - NA_ROWS):
        ws = int(np.clip(r0 - WIN_R // 2, 0, GRID_ROWS - NA_KROWS))
        r = r0 + np.arange(NA_ROWS)[:, None, None, None]
        cq = np.arange(GRID_W)[None, :, None, None]
        kr = ws + np.arange(NA_KROWS)[None, None, :, None]
        kc = np.arange(GRID_W)[None, None, None, :]
        rs = np.clip(r - WIN_R // 2, 0, GRID_ROWS - WIN_R)
        cs = np.clip(cq - WIN_C // 2, 0, GRID_W - WIN_C)
        ok = (kr >= rs) & (kr < rs + WIN_R) & (kc >= cs) & (kc < cs + WIN_C)
        dr = np.clip(kr - r + WIN_R - 1, 0, 2 * WIN_R - 2)
        dc = np.clip(kc - cq + WIN_C - 1, 0, 2 * WIN_C - 2)
        shape = (NA_ROWS, GRID_W, NA_KROWS, GRID_W)
        dr, dc, ok = (np.broadcast_to(t, shape).reshape(NA_ROWS * GRID_W, NA_KROWS * GRID_W) for t in (dr, dc, ok))
        tables.append(jnp.where(ok[None], rpb[:, dr, dc].astype(F32) * LOG2E, NEG))
    return jnp.stack(tables, axis=1)


def natten(qkv, k_ctx, v_ctx, rpb):
    bias = _natten_bias(rpb)
    nrb = GRID_ROWS // NA_ROWS
    tq = NA_ROWS * GRID_W
    nkeys = NA_KROWS * GRID_W

    def case(rb):
        return jnp.where(rb == 0, 0, jnp.where(rb == nrb - 1, 2, 1))

    ng = NH_C // NA_HEADS
    wd = NA_HEADS * DH_C
    return pl.pallas_call(
        functools.partial(_natten_kernel, scale=DH_C ** -0.5),
        grid=(N_LAT_B, ng, nrb),
        in_specs=[
            pl.BlockSpec((1, tq, wd), lambda b, h, rb: (1 + b, rb, h)),
            pl.BlockSpec((1, LAT_T, wd), lambda b, h, rb: (1 + b, 0, ng + h)),
            pl.BlockSpec((1, LAT_T, wd), lambda b, h, rb: (1 + b, 0, 2 * ng + h)),
            pl.BlockSpec((1, PAST, wd), lambda b, h, rb: (b, 0, h)),
            pl.BlockSpec((1, PAST, wd), lambda b, h, rb: (b, 0, h)),
            pl.BlockSpec((NA_HEADS, 1, tq, nkeys), lambda b, h, rb: (h, case(rb), 0, 0)),
        ],
        out_specs=pl.BlockSpec((1, tq, wd), lambda b, h, rb: (b, rb, h)),
        out_shape=jax.ShapeDtypeStruct((N_LAT_B, LAT_T, NH_C * DH_C), BF),
        compiler_params=_cparams(("parallel", "parallel", "arbitrary")), name="natten",
    )(qkv, qkv, qkv, k_ctx, v_ctx, bias)


def _mlstm_kernel(q_ref, k_ref, v_ref, og_ref, gates_ref, c0_ref, nm0_ref, ghn_ref,
                  y_ref, cn_ref, nmn_ref, hf_sc, hb_sc, c_sc, *, nc):
    L = CHUNK
    row = lax.broadcasted_iota(jnp.int32, (L, L), 0)
    col = lax.broadcasted_iota(jnp.int32, (L, L), 1)
    eye = row == col
    nt = (((1,), (1,)), ((), ()))
    tn = (((0,), (0,)), ((), ()))

    def to_col(r):
        return jnp.sum(jnp.where(eye, jnp.broadcast_to(r, (L, L)), 0.0), axis=1, keepdims=True)

    def chunk(c, d, n, m, reverse):
        sl = pl.ds(pl.multiple_of(c * L, L), L)
        g = gates_ref[0, 0, c]
        ic_row = g[2 * d:2 * d + 1, :]
        lf_row = g[2 * d + 1:2 * d + 2, :]
        q = q_ref[0, sl, :]
        k = k_ref[0, sl, :] * (DQK_A ** -0.5)
        v = v_ref[0, sl, :].astype(BF)
        mask = (col >= row) if reverse else (col <= row)
        mask_t = (row >= col) if reverse else (row <= col)
        lf_b = jnp.broadcast_to(lf_row, (L, L))
        b_col = jnp.sum(jnp.where(mask, lf_b, 0.0), axis=1, keepdims=True)
        lf_col = jnp.sum(jnp.where(eye, lf_b, 0.0), axis=1, keepdims=True)
        b_row = jnp.sum(jnp.where(mask_t, lf_col, 0.0), axis=0, keepdims=True)
        ic_col = to_col(ic_row)
        dmat = jnp.where(mask, b_col - b_row + ic_row, NEG)
        inter = b_col + m
        m_t = jnp.maximum(inter, jnp.max(dmat, axis=1, keepdims=True))
        a = jnp.exp(inter - m_t)
        qb = q.astype(BF)
        s = lax.dot_general(qb, k.astype(BF), nt, preferred_element_type=F32) * jnp.exp(dmat - m_t)
        c_old = c_sc[d]
        num = a * jnp.dot(qb, c_old.astype(BF), preferred_element_type=F32)
        num += jnp.dot(s.astype(BF), v, preferred_element_type=F32)
        den = a * jnp.sum(q * n, axis=1, keepdims=True) + jnp.sum(s, axis=1, keepdims=True)
        h = num / jnp.maximum(jnp.abs(den), jnp.exp(-m_t))
        b_end = jnp.sum(lf_row, axis=1, keepdims=True)
        g_col = b_end - b_col + ic_col
        m_new = jnp.maximum(b_end + m, jnp.max(g_col, axis=0, keepdims=True))
        a_s = jnp.exp(b_end + m - m_new)
        kw = k * jnp.exp(g_col - m_new)
        c_sc[d] = a_s * c_old + lax.dot_general(kw.astype(BF), v, tn, preferred_element_type=F32)
        n_new = a_s * n + jnp.sum(kw, axis=0, keepdims=True)
        return h, sl, n_new, m_new

    c_sc[0] = c0_ref[0, 0, 0]
    c_sc[1] = c0_ref[0, 1, 0]
    init = (nm0_ref[0, 0, 0, 0:1, :], nm0_ref[0, 0, 0, 1:2, 0:1],
            nm0_ref[0, 1, 0, 0:1, :], nm0_ref[0, 1, 0, 1:2, 0:1])

    def body(i, carry):
        n_f, m_f, n_b, m_b = carry
        h, sl, n_f, m_f = chunk(i, 0, n_f, m_f, False)
        hf_sc[sl, :] = h
        h, sl, n_b, m_b = chunk(nc - 1 - i, 1, n_b, m_b, True)
        hb_sc[sl, :] = h
        return n_f, m_f, n_b, m_b

    n_f, m_f, n_b, m_b = lax.fori_loop(0, nc, body, init)

    cn_ref[0, 0, 0] = c_sc[0]
    cn_ref[0, 1, 0] = c_sc[1]
    for d, (n_d, m_d) in enumerate(((n_f, m_f), (n_b, m_b))):
        nmn_ref[0, d, 0] = jnp.concatenate(
            [n_d, jnp.broadcast_to(m_d, (1, DQK_A)), jnp.zeros((6, DQK_A), F32)], axis=0)

    rows = 256
    gain = ghn_ref[0]

    def out_body(i, _):
        sl = pl.ds(pl.multiple_of(i * rows, rows), rows)
        hs = hf_sc[sl, :] + hb_sc[sl, :]
        hn = hs * lax.rsqrt(jnp.mean(hs * hs, axis=-1, keepdims=True) + EPS) * gain
        gate = 1.0 / (1.0 + jnp.exp(-og_ref[0, sl, :]))
        y_ref[0, sl, :] = (gate * hn).astype(y_ref.dtype)
        return 0

    lax.fori_loop(0, (nc * L) // rows, out_body, 0)


def mlstm(proj, gates, c0, nm0, g_hn, *, nb, t_len, boff):
    nc = t_len // CHUNK
    kq = A_QK // DQK_A
    kv = 2 * A_QK // DV_A
    ko = (2 * A_QK + A_V) // DV_A
    return pl.pallas_call(
        functools.partial(_mlstm_kernel, nc=nc),
        grid=(nb, NH_A),
        in_specs=[
            pl.BlockSpec((1, t_len, DQK_A), lambda b, h: (boff + b, 0, h)),
            pl.BlockSpec((1, t_len, DQK_A), lambda b, h: (boff + b, 0, kq + h)),
            pl.BlockSpec((1, t_len, DV_A), lambda b, h: (boff + b, 0, kv + h)),
            pl.BlockSpec((1, t_len, DV_A), lambda b, h: (boff + b, 0, ko + h)),
            pl.BlockSpec((1, 1, nc, 4, CHUNK), lambda b, h: (b, h, 0, 0, 0)),
            pl.BlockSpec((1, 2, 1, DQK_A, DV_A), lambda b, h: (b, 0, h, 0, 0)),
            pl.BlockSpec((1, 2, 1, 8, DQK_A), lambda b, h: (b, 0, h, 0, 0)),
            pl.BlockSpec((1, 1, DV_A), lambda b, h: (h, 0, 0)),
        ],
        out_specs=[
            pl.BlockSpec((1, t_len, DV_A), lambda b, h: (b, 0, h)),
            pl.BlockSpec((1, 2, 1, DQK_A, DV_A), lambda b, h: (b, 0, h, 0, 0)),
            pl.BlockSpec((1, 2, 1, 8, DQK_A), lambda b, h: (b, 0, h, 0, 0)),
        ],
        out_shape=[
            jax.ShapeDtypeStruct((nb, t_len, A_V), BF),
            jax.ShapeDtypeStruct((nb, 2, NH_A, DQK_A, DV_A), F32),
            jax.ShapeDtypeStruct((nb, 2, NH_A, 8, DQK_A), F32),
        ],
        scratch_shapes=[pltpu.VMEM((t_len, DV_A), F32), pltpu.VMEM((t_len, DV_A), F32),
                        pltpu.VMEM((2, DQK_A, DV_A), F32)],
        compiler_params=_cparams(("parallel", "parallel")), name="mlstm",
    )(proj, proj, proj, proj, gates, c0, nm0, g_hn.reshape(NH_A, 1, DV_A))


def _w1_prep_kernel(w_ref, p_ref, o_ref):
    gw = 2 * MOE_GRP
    for c in range(w_ref.shape[2] // gw):
        blk = w_ref[0, :, c * gw:(c + 1) * gw].astype(BF)
        o_ref[0, :, c * gw:(c + 1) * gw] = jnp.dot(blk, p_ref[...], preferred_element_type=F32).astype(BF)


def _deinterleave_perm():
    gw = 2 * MOE_GRP
    j = np.arange(gw)
    src = np.where(j < MOE_GRP, 2 * j, 2 * (j - MOE_GRP) + 1)
    perm = np.zeros((gw, gw), np.float32)
    perm[src, j] = 1.0
    return jnp.asarray(perm, BF)


def w1_prep(w1):
    ne, d, n2 = w1.shape
    tn = 1024
    gw = 2 * MOE_GRP
    return pl.pallas_call(
        _w1_prep_kernel,
        grid=(ne, n2 // tn),
        in_specs=[pl.BlockSpec((1, d, tn), lambda e, j: (e, 0, j)), pl.BlockSpec((gw, gw), lambda e, j: (0, 0))],
        out_specs=pl.BlockSpec((1, d, tn), lambda e, j: (e, 0, j)),
        out_shape=jax.ShapeDtypeStruct((ne, d, n2), BF),
        compiler_params=_cparams(("parallel", "parallel")), name="w1_prep",
    )(w1, _deinterleave_perm())


def _cast_kernel(x_ref, o_ref):
    o_ref[...] = x_ref[...].astype(o_ref.dtype)


def cast_bf16(x, rows):
    ne, r, c = x.shape
    return pl.pallas_call(
        _cast_kernel,
        grid=(ne, r // rows),
        in_specs=[pl.BlockSpec((1, rows, c), lambda e, j: (e, j, 0))],
        out_specs=pl.BlockSpec((1, rows, c), lambda e, j: (e, j, 0)),
        out_shape=jax.ShapeDtypeStruct(x.shape, BF),
        compiler_params=_cparams(("parallel", "parallel")), name="cast_bf16",
    )(x)


def _moe_kernel(blk_e_ref, nused_ref, x_ref, w1_ref, b1_ref, w2_ref, b2_ref, o_ref):
    b = pl.program_id(0)
    f = pl.program_id(1)

    @pl.when(f == 0)
    def _():
        o_ref[...] = jnp.broadcast_to(b2_ref[0], o_ref.shape)

    @pl.when(b < nused_ref[0])
    def _():
        x = x_ref[...]
        gw = 2 * MOE_GRP
        part = None
        for c in range(MOE_TF // MOE_GRP):
            hh = jnp.dot(x, w1_ref[0, :, c * gw:(c + 1) * gw], preferred_element_type=F32)
            hh = hh + b1_ref[0, :, c * gw:(c + 1) * gw]
            glu = jnp.minimum(hh[:, :MOE_GRP], SWIGLU_LIMIT)
            lin = jnp.clip(hh[:, MOE_GRP:], -SWIGLU_LIMIT, SWIGLU_LIMIT)
            act = glu * (1.0 / (1.0 + jnp.exp(-SWIGLU_ALPHA * glu))) * (lin + 1.0)
            p = jnp.dot(act.astype(BF), w2_ref[0, c * MOE_GRP:(c + 1) * MOE_GRP, :], preferred_element_type=F32)
            part = p if part is None else part + p
        o_ref[...] += part


def moe_experts(xb, blk_e, nused, w1p, b1p, w2, b2):
    nf = D_FF // MOE_TF
    d = D_MODEL

    def eidx(b, be):
        return be[b]

    def fidx(b, f, nu):
        return jnp.where(b < nu[0], f, nf - 1)

    grid_spec = pltpu.PrefetchScalarGridSpec(
        num_scalar_prefetch=2,
        grid=(MOE_NB, nf),
        in_specs=[
            pl.BlockSpec((MOE_BM, d), lambda b, f, be, nu: (jnp.minimum(b, nu[0] - 1), 0)),
            pl.BlockSpec((1, d, 2 * MOE_TF), lambda b, f, be, nu: (eidx(b, be), 0, fidx(b, f, nu))),
            pl.BlockSpec((1, 1, 2 * MOE_TF), lambda b, f, be, nu: (eidx(b, be), 0, fidx(b, f, nu))),
            pl.BlockSpec((1, MOE_TF, d), lambda b, f, be, nu: (eidx(b, be), fidx(b, f, nu), 0)),
            pl.BlockSpec((1, 1, d), lambda b, f, be, nu: (eidx(b, be), 0, 0)),
        ],
        out_specs=pl.BlockSpec((MOE_BM, d), lambda b, f, be, nu: (b, 0)),
    )
    return pl.pallas_call(
        _moe_kernel,
        grid_spec=grid_spec,
        out_shape=jax.ShapeDtypeStruct((MOE_NB * MOE_BM, d), F32),
        compiler_params=_cparams(("arbitrary", "arbitrary")), name="moe_experts",
    )(blk_e, nused, xb, w1p, b1p, w2, b2)


def moe(hdn, logits, w1, b1, w2, b2):
    n = hdn.shape[0]
    nk = n * TOP_K
    top_v, top_i = lax.top_k(logits, TOP_K)
    gate = jax.nn.softmax(top_v, axis=-1)
    e = top_i.reshape(-1).astype(jnp.int32)
    onehot = (e[:, None] == jnp.arange(N_EXP, dtype=jnp.int32)[None, :]).astype(jnp.int32)
    csum = jnp.cumsum(onehot, axis=0)
    rank = jnp.take_along_axis(csum, e[:, None], axis=1)[:, 0] - 1
    counts = csum[-1]
    padded = (counts + MOE_BM - 1) // MOE_BM * MOE_BM
    pad_end = jnp.cumsum(padded)
    pad_start = pad_end - padded
    dest = pad_start[e] + rank
    tok = jnp.arange(nk, dtype=jnp.int32) // TOP_K
    slot_tok = jnp.zeros((MOE_NB * MOE_BM,), jnp.int32).at[dest].set(tok)
    blk_e = jnp.minimum(
        jnp.searchsorted(pad_end, jnp.arange(MOE_NB, dtype=jnp.int32) * MOE_BM, side="right"), N_EXP - 1
    ).astype(jnp.int32)
    nused = (pad_end[-1] // MOE_BM).astype(jnp.int32).reshape(1)
    xb = jnp.take(hdn, slot_tok, axis=0)
    b1p = b1.reshape(N_EXP, D_FF // MOE_GRP, MOE_GRP, 2).transpose(0, 1, 3, 2).reshape(N_EXP, 1, 2 * D_FF)
    yb = moe_experts(xb, blk_e, nused, w1_prep(w1), b1p, cast_bf16(w2, MOE_TF), b2[:, None, :])
    y = jnp.take(yb, dest, axis=0).reshape(n, TOP_K, -1)
    return jnp.sum(y * gate[:, :, None], axis=1)


def _pad_cols(w, width):
    return jnp.pad(w, ((0, 0), (0, width - w.shape[1])))


def _adaln(cond, ada_w, ada_b):
    a = jnp.pad(jax.nn.silu(cond), ((0, 8 - cond.shape[0]), (0, 0)))
    mod = matmul(a, ada_w, bias=ada_b, tm=8, tn=1024)
    return mod[:cond.shape[0]].reshape(cond.shape[0], 6, D_MODEL)


def _layer0_mixer(x, mod, g_mix, w_in, b_gate, g_qa, w_qb, g_kva, w_kvb, g_qn, g_kn, g_hn, w_out, cache):
    st_c, st_n, st_m, c_ckv, c_kpe = cache
    o_gate = 2 * A_QK + 2 * A_V
    o_cq = o_gate + N_GATE
    w_gate = _pad_cols(w_in[:, o_gate:o_cq], AUX_W)
    hdn, gates = norm_mod(x, g_mix, mod, 0, aux_w=w_gate, aux_b=jnp.pad(b_gate, (0, AUX_W - N_GATE)))
    proj = matmul(hdn, w_in[:, :o_gate].astype(BF))
    w_b = _pad_cols(w_in[:, o_cq:], MLA_IN_W).astype(BF)
    proj_b = matmul(hdn, w_b, tn=MLA_IN_W)

    gt = GATE_CAP * jnp.tanh(gates[:, :N_GATE] / GATE_CAP)
    gt = gt.reshape(N_TOK, 4, NH_A)
    gt = jnp.stack([gt[:, 0], jax.nn.log_sigmoid(gt[:, 1]), gt[:, 2], jax.nn.log_sigmoid(gt[:, 3])], axis=1)

    def gate_layout(g, nb, t_len):
        g = g.reshape(nb, t_len // CHUNK, CHUNK, 4, NH_A)
        return g.transpose(0, 4, 1, 3, 2)

    zc = jnp.zeros((N_CTX_B, 2, NH_A, DQK_A, DV_A), F32)
    znm = jnp.zeros((N_CTX_B, 2, NH_A, 8, DQK_A), F32)
    y_ctx, new_c, new_nm = mlstm(proj.reshape(N_TOK // CTX_T, CTX_T, -1), gate_layout(gt[:N_CTX], N_CTX_B, CTX_T),
                                 zc, znm, g_hn, nb=N_CTX_B, t_len=CTX_T, boff=0)
    nm_lat = jnp.concatenate([st_n[:, :, :, None, :],
                              jnp.broadcast_to(st_m[:, :, :, None, None], (N_LAT_B, 2, NH_A, 1, DQK_A)),
                              jnp.zeros((N_LAT_B, 2, NH_A, 6, DQK_A), F32)], axis=3)
    y_lat, _, _ = mlstm(proj.reshape(N_TOK // LAT_T, LAT_T, -1), gate_layout(gt[N_CTX:], N_LAT_B, LAT_T),
                        st_c, nm_lat, g_hn, nb=N_LAT_B, t_len=LAT_T, boff=1)
    y_a = jnp.concatenate([y_ctx.reshape(N_CTX, A_V), y_lat.reshape(N_LAT, A_V)], axis=0)
    new_n = new_nm[:, :, :, 0, :]
    new_m = new_nm[:, :, :, 1, 0]

    wq = w_qb.reshape(Q_LORA, NH_B, QK_B)
    wq = jnp.pad(wq, ((0, 0), (0, 0), (0, QK_B_PAD - QK_B))).reshape(Q_LORA, NH_B * QK_B_PAD).astype(BF)
    gq = jnp.tile(jnp.pad(g_qn, (0, QK_B_PAD - QK_B)), NH_B)
    tables = _rotary_tables()
    w_kvb_b = w_kvb.astype(BF)
    q = mla_q(proj_b, g_qa, wq, gq, tables)
    k_new, v_new, ckv_n = mla_kv(proj_b, 1, proj_b, (Q_LORA + KV_LORA) // 128, g_kva, w_kvb_b, g_kn, tables,
                                 normalize=True)
    k_old, v_old = mla_kv(c_ckv.reshape(N_LAT_B * PAST, KV_LORA), 0,
                          _pad_cols(c_kpe.reshape(N_LAT_B * PAST, ROPE), 128), 0, g_kva, w_kvb_b, g_kn, None,
                          normalize=False)
    kpe = proj_b[:N_CTX, Q_LORA + KV_LORA:Q_LORA + KV_LORA + ROPE]
    ident = lambda b, h: (b, h)
    lat = lambda b, h: (1 + b, h)
    kw, vw = NH_B * QK_B_PAD, NH_B * VH
    y_b_ctx = attention(
        q.reshape(N_TOK // CTX_T, CTX_T, kw), ident,
        [(k_new.reshape(N_TOK // CTX_T, CTX_T, kw), ident, v_new.reshape(N_TOK // CTX_T, CTX_T, vw), ident, CTX_T)],
        nb=N_CTX_B, nh=NH_B, tq_len=CTX_T, dk=QK_B_PAD, dv=VH, tq=CTX_T, q_scale=None)
    y_b_lat = attention(
        q.reshape(N_TOK // LAT_T, LAT_T, kw), lat,
        [(k_new.reshape(N_TOK // LAT_T, LAT_T, kw), lat, v_new.reshape(N_TOK // LAT_T, LAT_T, vw), lat, LAT_T),
         (k_old.reshape(N_LAT_B, PAST, kw), ident, v_old.reshape(N_LAT_B, PAST, vw), ident, PAST)],
        nb=N_LAT_B, nh=NH_B, tq_len=LAT_T, dk=QK_B_PAD, dv=VH, tq=256, q_scale=None)
    y_b = jnp.concatenate([y_b_ctx.reshape(N_CTX, -1), y_b_lat.reshape(N_LAT, -1)], axis=0)

    y = jnp.concatenate([y_a, y_b], axis=1)
    x = matmul(y, w_out.astype(BF), resid=x, mod=mod, gate_row=2)
    new = (new_c, new_n, new_m, ckv_n[:N_CTX].reshape(N_CTX_B, CTX_T, KV_LORA), kpe.reshape(N_CTX_B, CTX_T, ROPE))
    return x, new


def _layer1_mixer(x, mod, g_mix, w_qkv, g_qn, g_kn, rpb, w_out, cache):
    c_k, c_v = cache
    hd = NH_C * DH_C
    hdn = norm_mod(x, g_mix, mod, 0)
    gain = jnp.concatenate([jnp.tile(g_qn, NH_C), jnp.tile(g_kn, NH_C), jnp.ones((hd,), F32)])
    qkv = matmul(hdn, w_qkv.astype(BF), gain=gain, norm_group=DH_C, norm_div=float(DH_C), n_norm_cols=2 * hd)
    new_k = qkv[:N_CTX, hd:2 * hd].reshape(N_CTX_B, CTX_T, NH_C, DH_C)
    new_v = qkv[:N_CTX, 2 * hd:].reshape(N_CTX_B, CTX_T, NH_C, DH_C)
    qkv_c = qkv.reshape(N_TOK // CTX_T, CTX_T, 3 * hd)
    o_ctx = attention(
        qkv_c, lambda b, h: (b, h),
        [(qkv_c, lambda b, h: (b, NH_C + h), qkv_c, lambda b, h: (b, 2 * NH_C + h), CTX_T)],
        nb=N_CTX_B, nh=NH_C, tq_len=CTX_T, dk=DH_C, dv=DH_C, tq=CTX_T, q_scale=DH_C ** -0.5)
    o_lat = natten(qkv.reshape(N_TOK // LAT_T, LAT_T, 3 * hd), c_k.reshape(N_LAT_B, PAST, hd),
                   c_v.reshape(N_LAT_B, PAST, hd), rpb)
    o = jnp.concatenate([o_ctx.reshape(N_CTX, hd), o_lat.reshape(N_LAT, hd)], axis=0)
    x = matmul(o, w_out.astype(BF), resid=x, mod=mod, gate_row=2)
    return x, (new_k, new_v)


def _moe_sublayer(x, mod, g_moe, w_r, b_r, w1, b1, w2, b2):
    hdn, logits = norm_mod(x, g_moe, mod, 1, aux_w=_pad_cols(w_r, AUX_W), aux_b=jnp.pad(b_r, (0, AUX_W - N_EXP)))
    y = moe(hdn, logits[:, :N_EXP], w1, b1, w2, b2)
    ga = jnp.repeat(mod[:, 5, :], N_CTX, axis=0, total_repeat_length=N_TOK)
    return x + ga * y


def kernel(x_prompt, x_sample, state_l0_mlstm_C, state_l0_mlstm_n, state_l0_mlstm_m, cache_l0_mla_ckv, cache_l0_mla_kpe, cache_l1_na_k, cache_l1_na_v, c, c_ctx, l0_g_mix, l0_g_moe, l0_ada_w, l0_ada_b, l0_w_in, l0_b_gate, l0_g_qa, l0_w_qb, l0_g_kva, l0_w_kvb, l0_g_qn, l0_g_kn, l0_g_hn, l0_w_out, l0_w_router, l0_b_router, l0_w1, l0_b1, l0_w2, l0_b2, l1_g_mix, l1_g_moe, l1_ada_w, l1_ada_b, l1_w_qkv, l1_g_qn, l1_g_kn, l1_rpb, l1_w_out, l1_w_router, l1_b_router, l1_w1, l1_b1, l1_w2, l1_b2):
    x = jnp.concatenate([x_prompt.reshape(N_CTX, D_MODEL), x_sample.reshape(N_LAT, D_MODEL)], axis=0)
    cond = jnp.concatenate([c_ctx[None, :], c], axis=0)

    mod0 = _adaln(cond, l0_ada_w, l0_ada_b)
    x, (new_c, new_n, new_m, new_ckv, new_kpe) = _layer0_mixer(
        x, mod0, l0_g_mix, l0_w_in, l0_b_gate, l0_g_qa, l0_w_qb, l0_g_kva, l0_w_kvb, l0_g_qn, l0_g_kn, l0_g_hn,
        l0_w_out, (state_l0_mlstm_C, state_l0_mlstm_n, state_l0_mlstm_m, cache_l0_mla_ckv, cache_l0_mla_kpe))
    x = _moe_sublayer(x, mod0, l0_g_moe, l0_w_router, l0_b_router, l0_w1, l0_b1, l0_w2, l0_b2)

    mod1 = _adaln(cond, l1_ada_w, l1_ada_b)
    x, (new_k, new_v) = _layer1_mixer(x, mod1, l1_g_mix, l1_w_qkv, l1_g_qn, l1_g_kn, l1_rpb, l1_w_out,
                                      (cache_l1_na_k, cache_l1_na_v))
    x = _moe_sublayer(x, mod1, l1_g_moe, l1_w_router, l1_b_router, l1_w1, l1_b1, l1_w2, l1_b2)

    y_prompt = x[:N_CTX].reshape(N_CTX_B, CTX_T, D_MODEL)
    y_sample = x[N_CTX:].reshape(N_LAT_B, LAT_T, D_MODEL)
    return (y_prompt, y_sample, new_c, new_n, new_m, new_ckv, new_kpe, new_k, new_v)
```

```python
import functools

import numpy as np
import jax
import jax.numpy as jnp
from jax import lax
from jax.experimental import pallas as pl
from jax.experimental.pallas import tpu as pltpu

BF = jnp.bfloat16
F32 = jnp.float32

D_MODEL = 2048
N_CTX_B, CTX_T = 16, 256
N_LAT_B, LAT_T = 4, 4096
N_CTX = N_CTX_B * CTX_T
N_LAT = N_LAT_B * LAT_T
N_TOK = N_CTX + N_LAT
PAST = 512
GRID_W = 64
GRID_ROWS = LAT_T // GRID_W
EPS = 1e-6
NEG = -1e30

NH_A, DQK_A, DV_A = 4, 128, 256
A_QK, A_V = NH_A * DQK_A, NH_A * DV_A
N_GATE = 4 * NH_A
CHUNK = 64
GATE_CAP = 15.0
NH_B, Q_LORA, KV_LORA, NOPE, ROPE, VH = 8, 512, 512, 128, 64, 128
QK_B = NOPE + ROPE
QK_B_PAD = 256
ROPE_BASE = 10000.0
NH_C, DH_C = 16, 128
WIN_R, WIN_C = 8, 16
NA_ROWS = 4
NA_KROWS = 12
NA_HEADS = 2
LOG2E = 1.4426950408889634
N_EXP, TOP_K, D_FF = 32, 4, 2048
SWIGLU_ALPHA, SWIGLU_LIMIT = 1.702, 7.0
MOE_BM = 512
MOE_TF = 1024
MOE_GRP = 256
MOE_NB = (N_TOK * TOP_K + N_EXP * (MOE_BM - 1) + MOE_BM - 1) // MOE_BM
AUX_W = 128
MLA_IN_W = 1280

VMEM_LIMIT = 56 * 1024 * 1024


def _cparams(sem):
    return pltpu.CompilerParams(dimension_semantics=sem, vmem_limit_bytes=VMEM_LIMIT)


def _split_bf16(x):
    hi = x.astype(BF)
    lo = (x - hi.astype(F32)).astype(BF)
    return hi, lo


def _norm_mod_kernel(*refs, which, has_aux):
    if has_aux:
        x_ref, g_ref, mod_ref, w_ref, b_ref, o_ref, aux_ref = refs
    else:
        x_ref, g_ref, mod_ref, o_ref = refs
    x = x_ref[...]
    ms = jnp.mean(x * x, axis=-1, keepdims=True)
    y = x * lax.rsqrt(ms + EPS) * g_ref[...]
    sh = mod_ref[0, 3 * which:3 * which + 1, :]
    sc = mod_ref[0, 3 * which + 1:3 * which + 2, :]
    h = y * (1.0 + sc) + sh
    o_ref[...] = h.astype(BF)
    if has_aux:
        h_hi, h_lo = _split_bf16(h)
        w_hi, w_lo = _split_bf16(w_ref[...])
        acc = jnp.dot(h_hi, w_hi, preferred_element_type=F32)
        acc += jnp.dot(h_hi, w_lo, preferred_element_type=F32)
        acc += jnp.dot(h_lo, w_hi, preferred_element_type=F32)
        aux_ref[...] = acc + b_ref[...]


def norm_mod(x, g, mod, which, aux_w=None, aux_b=None, tm=512):
    n, d = x.shape
    has_aux = aux_w is not None
    in_specs = [
        pl.BlockSpec((tm, d), lambda i: (i, 0)),
        pl.BlockSpec((1, d), lambda i: (0, 0)),
        pl.BlockSpec((1, 6, d), lambda i: ((i * tm) // N_CTX, 0, 0)),
    ]
    args = [x, g.reshape(1, d), mod]
    out_shape = [jax.ShapeDtypeStruct((n, d), BF)]
    out_specs = [pl.BlockSpec((tm, d), lambda i: (i, 0))]
    if has_aux:
        in_specs += [pl.BlockSpec((d, AUX_W), lambda i: (0, 0)), pl.BlockSpec((1, AUX_W), lambda i: (0, 0))]
        args += [aux_w, aux_b.reshape(1, AUX_W)]
        out_shape.append(jax.ShapeDtypeStruct((n, AUX_W), F32))
        out_specs.append(pl.BlockSpec((tm, AUX_W), lambda i: (i, 0)))
    res = pl.pallas_call(
        functools.partial(_norm_mod_kernel, which=which, has_aux=has_aux),
        grid=(n // tm,), in_specs=in_specs, out_specs=out_specs, out_shape=out_shape,
        compiler_params=_cparams(("parallel",)), name="norm_mod",
    )(*args)
    return res if has_aux else res[0]


def _group_rmsnorm(acc, gain, group, div):
    parts = []
    for c in range(acc.shape[1] // group):
        blk = acc[:, c * group:(c + 1) * group]
        ms = jnp.sum(blk * blk, axis=-1, keepdims=True) * (1.0 / div)
        parts.append(blk * lax.rsqrt(ms + EPS) * gain[:, c * group:(c + 1) * group])
    return parts[0] if len(parts) == 1 else jnp.concatenate(parts, axis=-1)


def _mm_kernel(*refs, has_bias, gate_row, norm_group, norm_div, n_norm_tiles):
    a_ref, w_ref = refs[0], refs[1]
    pos = 2
    if has_bias:
        b_ref = refs[pos]; pos += 1
    if gate_row is not None:
        x_ref, mod_ref = refs[pos], refs[pos + 1]; pos += 2
    if norm_group:
        gain_ref = refs[pos]; pos += 1
    o_ref = refs[pos]

    acc = jnp.dot(a_ref[...].astype(BF), w_ref[...].astype(BF), preferred_element_type=F32)
    if has_bias:
        acc = acc + b_ref[...]
    if gate_row is not None:
        acc = x_ref[...] + mod_ref[0, gate_row:gate_row + 1, :] * acc
    if norm_group:
        j = pl.program_id(1)

        @pl.when(j < n_norm_tiles)
        def _():
            o_ref[...] = _group_rmsnorm(acc, gain_ref[...], norm_group, norm_div).astype(o_ref.dtype)

        @pl.when(j >= n_norm_tiles)
        def _():
            o_ref[...] = acc.astype(o_ref.dtype)
    else:
        o_ref[...] = acc.astype(o_ref.dtype)


def matmul(a, w, *, bias=None, resid=None, mod=None, gate_row=None, gain=None, norm_group=0,
           norm_div=1.0, n_norm_cols=None, out_dtype=F32, tm=1024, tn=512):
    m, k = a.shape
    n = w.shape[1]
    tm = min(tm, m)
    tn = min(tn, n)
    assert m % tm == 0 and n % tn == 0
    in_specs = [pl.BlockSpec((tm, k), lambda i, j: (i, 0)), pl.BlockSpec((k, tn), lambda i, j: (0, j))]
    args = [a, w]
    if bias is not None:
        in_specs.append(pl.BlockSpec((1, tn), lambda i, j: (0, j)))
        args.append(bias.reshape(1, n))
    if gate_row is not None:
        in_specs.append(pl.BlockSpec((tm, tn), lambda i, j: (i, j)))
        in_specs.append(pl.BlockSpec((1, 6, tn), lambda i, j: ((i * tm) // N_CTX, 0, j)))
        args += [resid, mod]
    n_norm_tiles = 0
    if norm_group:
        n_norm_cols = n if n_norm_cols is None else n_norm_cols
        assert n_norm_cols % tn == 0 and tn % norm_group == 0
        n_norm_tiles = n_norm_cols // tn
        in_specs.append(pl.BlockSpec((1, tn), lambda i, j: (0, j)))
        args.append(gain.reshape(1, n))
    return pl.pallas_call(
        functools.partial(_mm_kernel, has_bias=bias is not None, gate_row=gate_row, norm_group=norm_group,
                          norm_div=norm_div, n_norm_tiles=n_norm_tiles),
        grid=(m // tm, n // tn), in_specs=in_specs,
        out_specs=pl.BlockSpec((tm, tn), lambda i, j: (i, j)),
        out_shape=jax.ShapeDtypeStruct((m, n), out_dtype),
        compiler_params=_cparams(("parallel", "parallel")), name="matmul",
    )(*args)


ATTN_CHUNK = 512


def _attn_kernel(*refs, q_scale, n_src):
    q_ref, o_ref = refs[0], refs[-1]
    q = q_ref[0]
    if q_scale is not None:
        q = q * (q_scale * LOG2E)
    q = q.astype(BF)
    nt = (((1,), (1,)), ((), ()))
    scores, values = [], []
    for i in range(n_src):
        k_ref, v_ref = refs[1 + 2 * i], refs[2 + 2 * i]
        tk = k_ref.shape[1]
        ch = min(ATTN_CHUNK, tk)
        for c in range(tk // ch):
            k = k_ref[0, c * ch:(c + 1) * ch, :].astype(BF)
            scores.append(lax.dot_general(q, k, nt, preferred_element_type=F32))
            values.append((v_ref, c * ch, ch))
    m = functools.reduce(jnp.maximum, [jnp.max(s, axis=1, keepdims=True) for s in scores])
    l, acc = None, None
    for s, (v_ref, start, ch) in zip(scores, values):
        p = jnp.exp2(s - m)
        ps = jnp.sum(p, axis=1, keepdims=True)
        pv = jnp.dot(p.astype(BF), v_ref[0, start:start + ch, :].astype(BF), preferred_element_type=F32)
        l = ps if l is None else l + ps
        acc = pv if acc is None else acc + pv
    o_ref[0] = (acc / l).astype(o_ref.dtype)


def attention(q, q_map, kv_sources, *, nb, nh, tq_len, dk, dv, tq, q_scale):
    in_specs = [pl.BlockSpec((1, tq, dk), lambda b, h, qi: (q_map(b, h)[0], qi, q_map(b, h)[1]))]
    args = [q]
    for k, k_map, v, v_map, n_keys in kv_sources:
        in_specs.append(pl.BlockSpec((1, n_keys, dk), lambda b, h, qi, f=k_map: (f(b, h)[0], 0, f(b, h)[1])))
        in_specs.append(pl.BlockSpec((1, n_keys, dv), lambda b, h, qi, f=v_map: (f(b, h)[0], 0, f(b, h)[1])))
        args += [k, v]
    return pl.pallas_call(
        functools.partial(_attn_kernel, q_scale=q_scale, n_src=len(kv_sources)),
        grid=(nb, nh, tq_len // tq), in_specs=in_specs,
        out_specs=pl.BlockSpec((1, tq, dv), lambda b, h, qi: (b, qi, h)),
        out_shape=jax.ShapeDtypeStruct((nb, tq_len, nh * dv), BF),
        compiler_params=_cparams(("parallel", "parallel", "arbitrary")), name="attention",
    )(*args)


def _rotate(x, cos, sin_a, sin_b):
    quarter = ROPE // 4
    return x * cos + pltpu.roll(x, 128 - quarter, 1) * sin_a + pltpu.roll(x, quarter, 1) * sin_b


def _mla_q_kernel(cq_ref, gqa_ref, wq_ref, gq_ref, cos_ref, sa_ref, sb_ref, o_ref, *, out_scale):
    cq = cq_ref[...]
    cqn = cq * lax.rsqrt(jnp.mean(cq * cq, axis=-1, keepdims=True) + EPS) * gqa_ref[...]
    q = jnp.dot(cqn.astype(BF), wq_ref[...], preferred_element_type=F32)
    cos, sa, sb = cos_ref[...], sa_ref[...], sb_ref[...]
    for h in range(NH_B):
        blk = q[:, h * QK_B_PAD:(h + 1) * QK_B_PAD]
        r = lax.rsqrt(jnp.sum(blk * blk, axis=-1, keepdims=True) * (1.0 / QK_B) + EPS)
        y = blk * r * gq_ref[:, h * QK_B_PAD:(h + 1) * QK_B_PAD]
        o_ref[:, h * QK_B_PAD:h * QK_B_PAD + NOPE] = (y[:, :NOPE] * out_scale).astype(o_ref.dtype)
        o_ref[:, h * QK_B_PAD + NOPE:(h + 1) * QK_B_PAD] = (
            _rotate(y[:, NOPE:], cos, sa, sb) * out_scale).astype(o_ref.dtype)


def mla_q(proj_b, g_qa, wq, gq, tables, tm=512):
    n = proj_b.shape[0]
    width = NH_B * QK_B_PAD
    tab = pl.BlockSpec((tm, 128), lambda i: (i, 0))
    return pl.pallas_call(
        functools.partial(_mla_q_kernel, out_scale=QK_B ** -0.5 * LOG2E),
        grid=(n // tm,),
        in_specs=[pl.BlockSpec((tm, Q_LORA), lambda i: (i, 0)), pl.BlockSpec((1, Q_LORA), lambda i: (0, 0)),
                  pl.BlockSpec((Q_LORA, width), lambda i: (0, 0)), pl.BlockSpec((1, width), lambda i: (0, 0)),
                  tab, tab, tab],
        out_specs=pl.BlockSpec((tm, width), lambda i: (i, 0)),
        out_shape=jax.ShapeDtypeStruct((n, width), BF),
        compiler_params=_cparams(("parallel",)), name="mla_q",
    )(proj_b, g_qa.reshape(1, Q_LORA), wq, gq.reshape(1, width), *tables)


def _mla_kv_kernel(*refs, normalize, rotate):
    if rotate:
        ckv_ref, kpe_ref, gkva_ref, w_ref, gkn_ref, gkr_ref, cos_ref, sa_ref, sb_ref = refs[:9]
        outs = refs[9:]
    else:
        ckv_ref, kpe_ref, gkva_ref, w_ref, gkn_ref, gkr_ref = refs[:6]
        outs = refs[6:]
    k_ref, v_ref = outs[0], outs[1]
    c = ckv_ref[...]
    if normalize:
        c = c * lax.rsqrt(jnp.mean(c * c, axis=-1, keepdims=True) + EPS) * gkva_ref[...]
        outs[2][...] = c
    kv = jnp.dot(c.astype(BF), w_ref[...], preferred_element_type=F32)
    kpe = kpe_ref[...]
    ss_pe = jnp.sum(kpe * kpe, axis=-1, keepdims=True)
    for h in range(NH_B):
        kn = kv[:, h * (NOPE + VH):h * (NOPE + VH) + NOPE]
        r = lax.rsqrt((jnp.sum(kn * kn, axis=-1, keepdims=True) + ss_pe) * (1.0 / QK_B) + EPS)
        kr = kpe * r * gkr_ref[...]
        if rotate:
            kr = _rotate(kr, cos_ref[...], sa_ref[...], sb_ref[...])
        k_ref[:, h * QK_B_PAD:h * QK_B_PAD + NOPE] = (kn * r * gkn_ref[...]).astype(k_ref.dtype)
        k_ref[:, h * QK_B_PAD + NOPE:(h + 1) * QK_B_PAD] = kr.astype(k_ref.dtype)
        v_ref[:, h * VH:(h + 1) * VH] = kv[:, h * (NOPE + VH) + NOPE:(h + 1) * (NOPE + VH)].astype(v_ref.dtype)


def mla_kv(ckv_src, ckv_blk, kpe_src, kpe_blk, g_kva, w_kvb, g_kn, tables, *, normalize, tm=512):
    n = ckv_src.shape[0]
    rotate = tables is not None
    kw, vw = NH_B * QK_B_PAD, NH_B * VH
    in_specs = [pl.BlockSpec((tm, KV_LORA), lambda i: (i, ckv_blk)), pl.BlockSpec((tm, 128), lambda i: (i, kpe_blk)),
                pl.BlockSpec((1, KV_LORA), lambda i: (0, 0)), pl.BlockSpec((KV_LORA, NH_B * (NOPE + VH)), lambda i: (0, 0)),
                pl.BlockSpec((1, NOPE), lambda i: (0, 0)), pl.BlockSpec((1, 128), lambda i: (0, 0))]
    args = [ckv_src, kpe_src, g_kva.reshape(1, KV_LORA), w_kvb, g_kn[:NOPE].reshape(1, NOPE),
            jnp.pad(g_kn[NOPE:], (0, 128 - ROPE)).reshape(1, 128)]
    if rotate:
        in_specs += [pl.BlockSpec((tm, 128), lambda i: (i, 0))] * 3
        args += list(tables)
    out_shape = [jax.ShapeDtypeStruct((n, kw), BF), jax.ShapeDtypeStruct((n, vw), BF)]
    out_specs = [pl.BlockSpec((tm, kw), lambda i: (i, 0)), pl.BlockSpec((tm, vw), lambda i: (i, 0))]
    if normalize:
        out_shape.append(jax.ShapeDtypeStruct((n, KV_LORA), F32))
        out_specs.append(pl.BlockSpec((tm, KV_LORA), lambda i: (i, 0)))
    return pl.pallas_call(
        functools.partial(_mla_kv_kernel, normalize=normalize, rotate=rotate),
        grid=(n // tm,), in_specs=in_specs, out_specs=out_specs, out_shape=out_shape,
        compiler_params=_cparams(("parallel",)), name="mla_kv",
    )(*args)


def _rotary_tables():
    half, quarter = ROPE // 2, ROPE // 4
    inv = ROPE_BASE ** (-jnp.arange(quarter, dtype=F32) * 2.0 / half)
    t = jnp.arange(LAT_T)
    lane = np.arange(128)
    is_s2 = ((lane % half) >= quarter) & (lane < ROPE)
    is_s1 = ((lane % half) < quarter) & (lane < ROPE)
    pos = jnp.where((lane < half)[None, :], (t // GRID_W)[:, None], (t % GRID_W)[:, None]).astype(F32)
    ang = pos * inv[lane % quarter][None, :]
    live = (lane < ROPE)[None, :]
    cos = jnp.where(live, jnp.cos(ang), 1.0)
    sin = jnp.sin(ang)
    sin_a = jnp.where(is_s1[None, :], -sin, 0.0)
    sin_b = jnp.where(is_s2[None, :], sin, 0.0)

    def full(tab, fill):
        return jnp.concatenate([jnp.full((N_CTX, 128), fill, F32), jnp.tile(tab, (N_LAT_B, 1))], axis=0)

    return full(cos, 1.0), full(sin_a, 0.0), full(sin_b, 0.0)


def _natten_kernel(q_ref, k_ref, v_ref, kc_ref, vc_ref, bias_ref, o_ref, *, scale):
    rb = pl.program_id(2)
    ws = jnp.clip(rb * NA_ROWS - WIN_R // 2, 0, GRID_ROWS - NA_KROWS)
    start = pl.multiple_of(ws * GRID_W, GRID_W)
    nkeys = NA_KROWS * GRID_W
    nt = (((1,), (1,)), ((), ()))
    for h in range(NA_HEADS):
        cs = slice(h * DH_C, (h + 1) * DH_C)
        q = (q_ref[0, :, cs] * (scale * LOG2E)).astype(BF)
        kw = k_ref[0, pl.ds(start, nkeys), cs].astype(BF)
        vw = v_ref[0, pl.ds(start, nkeys), cs].astype(BF)
        s_w = lax.dot_general(q, kw, nt, preferred_element_type=F32) + bias_ref[h, 0]
        s_c = lax.dot_general(q, kc_ref[0, :, cs].astype(BF), nt, preferred_element_type=F32)
        m = jnp.maximum(jnp.max(s_w, axis=1, keepdims=True), jnp.max(s_c, axis=1, keepdims=True))
        p_w = jnp.exp2(s_w - m)
        p_c = jnp.exp2(s_c - m)
        l = jnp.sum(p_w, axis=1, keepdims=True) + jnp.sum(p_c, axis=1, keepdims=True)
        o = jnp.dot(p_w.astype(BF), vw, preferred_element_type=F32)
        o += jnp.dot(p_c.astype(BF), vc_ref[0, :, cs].astype(BF), preferred_element_type=F32)
        o_ref[0, :, cs] = (o / l).astype(o_ref.dtype)


def _natten_bias(rpb):
    n_dr, n_dc = 2 * WIN_R - 1, 2 * WIN_C - 1
    cq = np.arange(GRID_W)[:, None]
    kc = np.arange(GRID_W)[None, :]
    cs = np.clip(cq - WIN_C // 2, 0, GRID_W - WIN_C)
    col_ok = (kc >= cs) & (kc < cs + WIN_C)
    dc = np.clip(kc - cq + WIN_C - 1, 0, n_dc - 1)
    col_sel = (dc[:, :, None] == np.arange(n_dc)).astype(np.float32)
    row_sel, row_ok = [], []
    for r0 in (0, 2 * NA_ROWS, GRID_ROWS - NA_ROWS):
        ws = int(np.clip(r0 - WIN_R // 2, 0, GRID_ROWS - NA_KROWS))
        r = r0 + np.arange(NA_ROWS)[:, None]
        kr = ws + np.arange(NA_KROWS)[None, :]
        rs = np.clip(r - WIN_R // 2, 0, GRID_ROWS - WIN_R)
        row_ok.append((kr >= rs) & (kr < rs + WIN_R))
        dr = np.clip(kr - r + WIN_R - 1, 0, n_dr - 1)
        row_sel.append((dr[:, :, None] == np.arange(n_dr)).astype(np.float32))
    row_sel, row_ok = np.stack(row_sel), np.stack(row_ok)
    hi = lax.Precision.HIGHEST
    t = jnp.einsum("hrc,sijr->hsijc", rpb.astype(F32), row_sel, precision=hi)
    t = jnp.einsum("hsijc,qkc->hsiqjk", t, col_sel, precision=hi)
    ok = row_ok[:, :, None, :, None] & col_ok[None, None, :, None, :]
    t = jnp.where(ok[None], t * LOG2E, NEG)
    return t.reshape(NH_C, 3, NA_ROWS * GRID_W, NA_KROWS * GRID_W)


def natten(qkv, k_ctx, v_ctx, rpb):
    bias = _natten_bias(rpb)
    nrb = GRID_ROWS // NA_ROWS
    tq = NA_ROWS * GRID_W
    nkeys = NA_KROWS * GRID_W

    def case(rb):
        return jnp.where(rb == 0, 0, jnp.where(rb == nrb - 1, 2, 1))

    ng = NH_C // NA_HEADS
    wd = NA_HEADS * DH_C
    return pl.pallas_call(
        functools.partial(_natten_kernel, scale=DH_C ** -0.5),
        grid=(N_LAT_B, ng, nrb),
        in_specs=[
            pl.BlockSpec((1, tq, wd), lambda b, h, rb: (1 + b, rb, h)),
            pl.BlockSpec((1, LAT_T, wd), lambda b, h, rb: (1 + b, 0, ng + h)),
            pl.BlockSpec((1, LAT_T, wd), lambda b, h, rb: (1 + b, 0, 2 * ng + h)),
            pl.BlockSpec((1, PAST, wd), lambda b, h, rb: (b, 0, h)),
            pl.BlockSpec((1, PAST, wd), lambda b, h, rb: (b, 0, h)),
            pl.BlockSpec((NA_HEADS, 1, tq, nkeys), lambda b, h, rb: (h, case(rb), 0, 0)),
        ],
        out_specs=pl.BlockSpec((1, tq, wd), lambda b, h, rb: (b, rb, h)),
        out_shape=jax.ShapeDtypeStruct((N_LAT_B, LAT_T, NH_C * DH_C), BF),
        compiler_params=_cparams(("parallel", "parallel", "arbitrary")), name="natten",
    )(qkv, qkv, qkv, k_ctx, v_ctx, bias)


def _mlstm_kernel(q_ref, k_ref, v_ref, og_ref, gates_ref, c0_ref, nm0_ref, ghn_ref,
                  y_ref, cn_ref, nmn_ref, hf_sc, hb_sc, c_sc, *, nc):
    L = CHUNK
    row = lax.broadcasted_iota(jnp.int32, (L, L), 0)
    col = lax.broadcasted_iota(jnp.int32, (L, L), 1)
    eye = row == col
    nt = (((1,), (1,)), ((), ()))
    tn = (((0,), (0,)), ((), ()))

    def to_col(r):
        return jnp.sum(jnp.where(eye, jnp.broadcast_to(r, (L, L)), 0.0), axis=1, keepdims=True)

    def chunk(c, d, n, m, reverse):
        sl = pl.ds(pl.multiple_of(c * L, L), L)
        g = gates_ref[0, 0, c]
        ic_row = g[2 * d:2 * d + 1, :]
        lf_row = g[2 * d + 1:2 * d + 2, :]
        q = q_ref[0, sl, :]
        k = k_ref[0, sl, :] * (DQK_A ** -0.5)
        v = v_ref[0, sl, :].astype(BF)
        mask = (col >= row) if reverse else (col <= row)
        mask_t = (row >= col) if reverse else (row <= col)
        lf_b = jnp.broadcast_to(lf_row, (L, L))
        b_col = jnp.sum(jnp.where(mask, lf_b, 0.0), axis=1, keepdims=True)
        lf_col = jnp.sum(jnp.where(eye, lf_b, 0.0), axis=1, keepdims=True)
        b_row = jnp.sum(jnp.where(mask_t, lf_col, 0.0), axis=0, keepdims=True)
        ic_col = to_col(ic_row)
        dmat = jnp.where(mask, b_col - b_row + ic_row, NEG)
        inter = b_col + m
        m_t = jnp.maximum(inter, jnp.max(dmat, axis=1, keepdims=True))
        a = jnp.exp(inter - m_t)
        qb = q.astype(BF)
        s = lax.dot_general(qb, k.astype(BF), nt, preferred_element_type=F32) * jnp.exp(dmat - m_t)
        c_old = c_sc[d]
        num = a * jnp.dot(qb, c_old.astype(BF), preferred_element_type=F32)
        num += jnp.dot(s.astype(BF), v, preferred_element_type=F32)
        den = a * jnp.sum(q * n, axis=1, keepdims=True) + jnp.sum(s, axis=1, keepdims=True)
        h = num / jnp.maximum(jnp.abs(den), jnp.exp(-m_t))
        b_end = jnp.sum(lf_row, axis=1, keepdims=True)
        g_col = b_end - b_col + ic_col
        m_new = jnp.maximum(b_end + m, jnp.max(g_col, axis=0, keepdims=True))
        a_s = jnp.exp(b_end + m - m_new)
        kw = k * jnp.exp(g_col - m_new)
        c_sc[d] = a_s * c_old + lax.dot_general(kw.astype(BF), v, tn, preferred_element_type=F32)
        n_new = a_s * n + jnp.sum(kw, axis=0, keepdims=True)
        return h, sl, n_new, m_new

    c_sc[0] = c0_ref[0, 0, 0]
    c_sc[1] = c0_ref[0, 1, 0]
    init = (nm0_ref[0, 0, 0, 0:1, :], nm0_ref[0, 0, 0, 1:2, 0:1],
            nm0_ref[0, 1, 0, 0:1, :], nm0_ref[0, 1, 0, 1:2, 0:1])

    def body(i, carry):
        n_f, m_f, n_b, m_b = carry
        h, sl, n_f, m_f = chunk(i, 0, n_f, m_f, False)
        hf_sc[sl, :] = h
        h, sl, n_b, m_b = chunk(nc - 1 - i, 1, n_b, m_b, True)
        hb_sc[sl, :] = h
        return n_f, m_f, n_b, m_b

    n_f, m_f, n_b, m_b = lax.fori_loop(0, nc, body, init, unroll=2)

    cn_ref[0, 0, 0] = c_sc[0]
    cn_ref[0, 1, 0] = c_sc[1]
    for d, (n_d, m_d) in enumerate(((n_f, m_f), (n_b, m_b))):
        nmn_ref[0, d, 0] = jnp.concatenate(
            [n_d, jnp.broadcast_to(m_d, (1, DQK_A)), jnp.zeros((6, DQK_A), F32)], axis=0)

    rows = 256
    gain = ghn_ref[0]

    def out_body(i, _):
        sl = pl.ds(pl.multiple_of(i * rows, rows), rows)
        hs = hf_sc[sl, :] + hb_sc[sl, :]
        hn = hs * lax.rsqrt(jnp.mean(hs * hs, axis=-1, keepdims=True) + EPS) * gain
        gate = 1.0 / (1.0 + jnp.exp(-og_ref[0, sl, :]))
        y_ref[0, sl, :] = (gate * hn).astype(y_ref.dtype)
        return 0

    lax.fori_loop(0, (nc * L) // rows, out_body, 0)


def mlstm(proj, gates, c0, nm0, g_hn, *, nb, t_len, boff):
    nc = t_len // CHUNK
    kq = A_QK // DQK_A
    kv = 2 * A_QK // DV_A
    ko = (2 * A_QK + A_V) // DV_A
    return pl.pallas_call(
        functools.partial(_mlstm_kernel, nc=nc),
        grid=(nb, NH_A),
        in_specs=[
            pl.BlockSpec((1, t_len, DQK_A), lambda b, h: (boff + b, 0, h)),
            pl.BlockSpec((1, t_len, DQK_A), lambda b, h: (boff + b, 0, kq + h)),
            pl.BlockSpec((1, t_len, DV_A), lambda b, h: (boff + b, 0, kv + h)),
            pl.BlockSpec((1, t_len, DV_A), lambda b, h: (boff + b, 0, ko + h)),
            pl.BlockSpec((1, 1, nc, 4, CHUNK), lambda b, h: (b, h, 0, 0, 0)),
            pl.BlockSpec((1, 2, 1, DQK_A, DV_A), lambda b, h: (b, 0, h, 0, 0)),
            pl.BlockSpec((1, 2, 1, 8, DQK_A), lambda b, h: (b, 0, h, 0, 0)),
            pl.BlockSpec((1, 1, DV_A), lambda b, h: (h, 0, 0)),
        ],
        out_specs=[
            pl.BlockSpec((1, t_len, DV_A), lambda b, h: (b, 0, h)),
            pl.BlockSpec((1, 2, 1, DQK_A, DV_A), lambda b, h: (b, 0, h, 0, 0)),
            pl.BlockSpec((1, 2, 1, 8, DQK_A), lambda b, h: (b, 0, h, 0, 0)),
        ],
        out_shape=[
            jax.ShapeDtypeStruct((nb, t_len, A_V), BF),
            jax.ShapeDtypeStruct((nb, 2, NH_A, DQK_A, DV_A), F32),
            jax.ShapeDtypeStruct((nb, 2, NH_A, 8, DQK_A), F32),
        ],
        scratch_shapes=[pltpu.VMEM((t_len, DV_A), F32), pltpu.VMEM((t_len, DV_A), F32),
                        pltpu.VMEM((2, DQK_A, DV_A), F32)],
        compiler_params=_cparams(("parallel", "parallel")), name="mlstm",
    )(proj, proj, proj, proj, gates, c0, nm0, g_hn.reshape(NH_A, 1, DV_A))


def _w1_prep_kernel(w_ref, p_ref, o_ref):
    gw = 2 * MOE_GRP
    for c in range(w_ref.shape[2] // gw):
        blk = w_ref[0, :, c * gw:(c + 1) * gw].astype(BF)
        o_ref[0, :, c * gw:(c + 1) * gw] = jnp.dot(blk, p_ref[...], preferred_element_type=F32).astype(BF)


def _deinterleave_perm():
    gw = 2 * MOE_GRP
    j = np.arange(gw)
    src = np.where(j < MOE_GRP, 2 * j, 2 * (j - MOE_GRP) + 1)
    perm = np.zeros((gw, gw), np.float32)
    perm[src, j] = 1.0
    return jnp.asarray(perm, BF)


def w1_prep(w1):
    ne, d, n2 = w1.shape
    tn = 1024
    gw = 2 * MOE_GRP
    return pl.pallas_call(
        _w1_prep_kernel,
        grid=(ne, n2 // tn),
        in_specs=[pl.BlockSpec((1, d, tn), lambda e, j: (e, 0, j)), pl.BlockSpec((gw, gw), lambda e, j: (0, 0))],
        out_specs=pl.BlockSpec((1, d, tn), lambda e, j: (e, 0, j)),
        out_shape=jax.ShapeDtypeStruct((ne, d, n2), BF),
        compiler_params=_cparams(("parallel", "parallel")), name="w1_prep",
    )(w1, _deinterleave_perm())


def _cast_kernel(x_ref, o_ref):
    o_ref[...] = x_ref[...].astype(o_ref.dtype)


def cast_bf16(x, rows):
    ne, r, c = x.shape
    return pl.pallas_call(
        _cast_kernel,
        grid=(ne, r // rows),
        in_specs=[pl.BlockSpec((1, rows, c), lambda e, j: (e, j, 0))],
        out_specs=pl.BlockSpec((1, rows, c), lambda e, j: (e, j, 0)),
        out_shape=jax.ShapeDtypeStruct(x.shape, BF),
        compiler_params=_cparams(("parallel", "parallel")), name="cast_bf16",
    )(x)


def _moe_kernel(blk_e_ref, nused_ref, x_ref, w1_ref, b1_ref, w2_ref, b2_ref, o_ref):
    b = pl.program_id(0)
    f = pl.program_id(1)

    @pl.when(f == 0)
    def _():
        o_ref[...] = jnp.broadcast_to(b2_ref[0], o_ref.shape)

    @pl.when(b < nused_ref[0])
    def _():
        x = x_ref[...]
        gw = 2 * MOE_GRP
        part = None
        for c in range(MOE_TF // MOE_GRP):
            hh = jnp.dot(x, w1_ref[0, :, c * gw:(c + 1) * gw], preferred_element_type=F32)
            hh = hh + b1_ref[0, :, c * gw:(c + 1) * gw]
            glu = jnp.minimum(hh[:, :MOE_GRP], SWIGLU_LIMIT)
            lin = jnp.clip(hh[:, MOE_GRP:], -SWIGLU_LIMIT, SWIGLU_LIMIT)
            act = glu * (1.0 / (1.0 + jnp.exp(-SWIGLU_ALPHA * glu))) * (lin + 1.0)
            p = jnp.dot(act.astype(BF), w2_ref[0, c * MOE_GRP:(c + 1) * MOE_GRP, :], preferred_element_type=F32)
            part = p if part is None else part + p
        o_ref[...] += part


def moe_experts(xb, blk_e, nused, w1p, b1p, w2, b2):
    nf = D_FF // MOE_TF
    d = D_MODEL

    def eidx(b, be):
        return be[b]

    def fidx(b, f, nu):
        return jnp.where(b < nu[0], f, nf - 1)

    grid_spec = pltpu.PrefetchScalarGridSpec(
        num_scalar_prefetch=2,
        grid=(MOE_NB, nf),
        in_specs=[
            pl.BlockSpec((MOE_BM, d), lambda b, f, be, nu: (jnp.minimum(b, nu[0] - 1), 0)),
            pl.BlockSpec((1, d, 2 * MOE_TF), lambda b, f, be, nu: (eidx(b, be), 0, fidx(b, f, nu))),
            pl.BlockSpec((1, 1, 2 * MOE_TF), lambda b, f, be, nu: (eidx(b, be), 0, fidx(b, f, nu))),
            pl.BlockSpec((1, MOE_TF, d), lambda b, f, be, nu: (eidx(b, be), fidx(b, f, nu), 0)),
            pl.BlockSpec((1, 1, d), lambda b, f, be, nu: (eidx(b, be), 0, 0)),
        ],
        out_specs=pl.BlockSpec((MOE_BM, d), lambda b, f, be, nu: (b, 0)),
    )
    return pl.pallas_call(
        _moe_kernel,
        grid_spec=grid_spec,
        out_shape=jax.ShapeDtypeStruct((MOE_NB * MOE_BM, d), F32),
        compiler_params=_cparams(("arbitrary", "arbitrary")), name="moe_experts",
    )(blk_e, nused, xb, w1p, b1p, w2, b2)


def moe(hdn, logits, w1, b1, w2, b2):
    n = hdn.shape[0]
    nk = n * TOP_K
    top_v, top_i = lax.top_k(logits, TOP_K)
    gate = jax.nn.softmax(top_v, axis=-1)
    e = top_i.reshape(-1).astype(jnp.int32)
    rb = 512
    onehot = (e[:, None] == jnp.arange(N_EXP, dtype=jnp.int32)[None, :]).astype(BF).reshape(nk // rb, rb, N_EXP)
    within = jnp.einsum("ts,bse->bte", jnp.tril(jnp.ones((rb, rb), BF)), onehot, preferred_element_type=F32)
    tot = within[:, -1, :]
    before = jnp.cumsum(tot, axis=0) - tot
    rank = jnp.sum((within + before[:, None, :]) * onehot.astype(F32), axis=-1).reshape(nk).astype(jnp.int32) - 1
    counts = (before[-1] + tot[-1]).astype(jnp.int32)
    padded = (counts + MOE_BM - 1) // MOE_BM * MOE_BM
    pad_end = jnp.cumsum(padded)
    pad_start = pad_end - padded
    dest = pad_start[e] + rank
    tok = jnp.arange(nk, dtype=jnp.int32) // TOP_K
    slot_tok = jnp.zeros((MOE_NB * MOE_BM,), jnp.int32).at[dest].set(tok)
    blk_e = jnp.minimum(
        jnp.searchsorted(pad_end, jnp.arange(MOE_NB, dtype=jnp.int32) * MOE_BM, side="right"), N_EXP - 1
    ).astype(jnp.int32)
    nused = (pad_end[-1] // MOE_BM).astype(jnp.int32).reshape(1)
    xb = jnp.take(hdn, slot_tok, axis=0)
    b1p = b1.reshape(N_EXP, D_FF // MOE_GRP, MOE_GRP, 2).transpose(0, 1, 3, 2).reshape(N_EXP, 1, 2 * D_FF)
    yb = moe_experts(xb, blk_e, nused, w1_prep(w1), b1p, cast_bf16(w2, MOE_TF), b2[:, None, :])
    dest = dest.reshape(n, TOP_K)
    y = None
    for k in range(TOP_K):
        yk = jnp.take(yb, dest[:, k], axis=0) * gate[:, k:k + 1]
        y = yk if y is None else y + yk
    return y


def _pad_cols(w, width):
    return jnp.pad(w, ((0, 0), (0, width - w.shape[1])))


def _adaln(cond, ada_w, ada_b):
    a = jnp.pad(jax.nn.silu(cond), ((0, 8 - cond.shape[0]), (0, 0)))
    mod = matmul(a, ada_w, bias=ada_b, tm=8, tn=1024)
    return mod[:cond.shape[0]].reshape(cond.shape[0], 6, D_MODEL)


def _layer0_mixer(x, mod, g_mix, w_in, b_gate, g_qa, w_qb, g_kva, w_kvb, g_qn, g_kn, g_hn, w_out, cache):
    st_c, st_n, st_m, c_ckv, c_kpe = cache
    o_gate = 2 * A_QK + 2 * A_V
    o_cq = o_gate + N_GATE
    w_gate = _pad_cols(w_in[:, o_gate:o_cq], AUX_W)
    hdn, gates = norm_mod(x, g_mix, mod, 0, aux_w=w_gate, aux_b=jnp.pad(b_gate, (0, AUX_W - N_GATE)))
    proj = matmul(hdn, w_in[:, :o_gate].astype(BF))
    w_b = _pad_cols(w_in[:, o_cq:], MLA_IN_W).astype(BF)
    proj_b = matmul(hdn, w_b, tn=MLA_IN_W)

    gt = GATE_CAP * jnp.tanh(gates[:, :N_GATE] / GATE_CAP)
    gt = gt.reshape(N_TOK, 4, NH_A)
    gt = jnp.stack([gt[:, 0], jax.nn.log_sigmoid(gt[:, 1]), gt[:, 2], jax.nn.log_sigmoid(gt[:, 3])], axis=1)

    def gate_layout(g, nb, t_len):
        g = g.reshape(nb, t_len // CHUNK, CHUNK, 4, NH_A)
        return g.transpose(0, 4, 1, 3, 2)

    zc = jnp.zeros((N_CTX_B, 2, NH_A, DQK_A, DV_A), F32)
    znm = jnp.zeros((N_CTX_B, 2, NH_A, 8, DQK_A), F32)
    y_ctx, new_c, new_nm = mlstm(proj.reshape(N_TOK // CTX_T, CTX_T, -1), gate_layout(gt[:N_CTX], N_CTX_B, CTX_T),
                                 zc, znm, g_hn, nb=N_CTX_B, t_len=CTX_T, boff=0)
    nm_lat = jnp.concatenate([st_n[:, :, :, None, :],
                              jnp.broadcast_to(st_m[:, :, :, None, None], (N_LAT_B, 2, NH_A, 1, DQK_A)),
                              jnp.zeros((N_LAT_B, 2, NH_A, 6, DQK_A), F32)], axis=3)
    y_lat, _, _ = mlstm(proj.reshape(N_TOK // LAT_T, LAT_T, -1), gate_layout(gt[N_CTX:], N_LAT_B, LAT_T),
                        st_c, nm_lat, g_hn, nb=N_LAT_B, t_len=LAT_T, boff=1)
    y_a = jnp.concatenate([y_ctx.reshape(N_CTX, A_V), y_lat.reshape(N_LAT, A_V)], axis=0)
    new_n = new_nm[:, :, :, 0, :]
    new_m = new_nm[:, :, :, 1, 0]

    wq = w_qb.reshape(Q_LORA, NH_B, QK_B)
    wq = jnp.pad(wq, ((0, 0), (0, 0), (0, QK_B_PAD - QK_B))).reshape(Q_LORA, NH_B * QK_B_PAD).astype(BF)
    gq = jnp.tile(jnp.pad(g_qn, (0, QK_B_PAD - QK_B)), NH_B)
    tables = _rotary_tables()
    w_kvb_b = w_kvb.astype(BF)
    q = mla_q(proj_b, g_qa, wq, gq, tables)
    k_new, v_new, ckv_n = mla_kv(proj_b, 1, proj_b, (Q_LORA + KV_LORA) // 128, g_kva, w_kvb_b, g_kn, tables,
                                 normalize=True)
    k_old, v_old = mla_kv(c_ckv.reshape(N_LAT_B * PAST, KV_LORA), 0,
                          _pad_cols(c_kpe.reshape(N_LAT_B * PAST, ROPE), 128), 0, g_kva, w_kvb_b, g_kn, None,
                          normalize=False)
    kpe = proj_b[:N_CTX, Q_LORA + KV_LORA:Q_LORA + KV_LORA + ROPE]
    ident = lambda b, h: (b, h)
    lat = lambda b, h: (1 + b, h)
    kw, vw = NH_B * QK_B_PAD, NH_B * VH
    y_b_ctx = attention(
        q.reshape(N_TOK // CTX_T, CTX_T, kw), ident,
        [(k_new.reshape(N_TOK // CTX_T, CTX_T, kw), ident, v_new.reshape(N_TOK // CTX_T, CTX_T, vw), ident, CTX_T)],
        nb=N_CTX_B, nh=NH_B, tq_len=CTX_T, dk=QK_B_PAD, dv=VH, tq=CTX_T, q_scale=None)
    y_b_lat = attention(
        q.reshape(N_TOK // LAT_T, LAT_T, kw), lat,
        [(k_new.reshape(N_TOK // LAT_T, LAT_T, kw), lat, v_new.reshape(N_TOK // LAT_T, LAT_T, vw), lat, LAT_T),
         (k_old.reshape(N_LAT_B, PAST, kw), ident, v_old.reshape(N_LAT_B, PAST, vw), ident, PAST)],
        nb=N_LAT_B, nh=NH_B, tq_len=LAT_T, dk=QK_B_PAD, dv=VH, tq=256, q_scale=None)
    y_b = jnp.concatenate([y_b_ctx.reshape(N_CTX, -1), y_b_lat.reshape(N_LAT, -1)], axis=0)

    y = jnp.concatenate([y_a, y_b], axis=1)
    x = matmul(y, w_out.astype(BF), resid=x, mod=mod, gate_row=2)
    new = (new_c, new_n, new_m, ckv_n[:N_CTX].reshape(N_CTX_B, CTX_T, KV_LORA), kpe.reshape(N_CTX_B, CTX_T, ROPE))
    return x, new


def _layer1_mixer(x, mod, g_mix, w_qkv, g_qn, g_kn, rpb, w_out, cache):
    c_k, c_v = cache
    hd = NH_C * DH_C
    hdn = norm_mod(x, g_mix, mod, 0)
    gain = jnp.concatenate([jnp.tile(g_qn, NH_C), jnp.tile(g_kn, NH_C), jnp.ones((hd,), F32)])
    qkv = matmul(hdn, w_qkv.astype(BF), gain=gain, norm_group=DH_C, norm_div=float(DH_C), n_norm_cols=2 * hd)
    new_k = qkv[:N_CTX, hd:2 * hd].reshape(N_CTX_B, CTX_T, NH_C, DH_C)
    new_v = qkv[:N_CTX, 2 * hd:].reshape(N_CTX_B, CTX_T, NH_C, DH_C)
    qkv_c = qkv.reshape(N_TOK // CTX_T, CTX_T, 3 * hd)
    o_ctx = attention(
        qkv_c, lambda b, h: (b, h),
        [(qkv_c, lambda b, h: (b, NH_C + h), qkv_c, lambda b, h: (b, 2 * NH_C + h), CTX_T)],
        nb=N_CTX_B, nh=NH_C, tq_len=CTX_T, dk=DH_C, dv=DH_C, tq=CTX_T, q_scale=DH_C ** -0.5)
    o_lat = natten(qkv.reshape(N_TOK // LAT_T, LAT_T, 3 * hd), c_k.reshape(N_LAT_B, PAST, hd),
                   c_v.reshape(N_LAT_B, PAST, hd), rpb)
    o = jnp.concatenate([o_ctx.reshape(N_CTX, hd), o_lat.reshape(N_LAT, hd)], axis=0)
    x = matmul(o, w_out.astype(BF), resid=x, mod=mod, gate_row=2)
    return x, (new_k, new_v)


def _moe_sublayer(x, mod, g_moe, w_r, b_r, w1, b1, w2, b2):
    hdn, logits = norm_mod(x, g_moe, mod, 1, aux_w=_pad_cols(w_r, AUX_W), aux_b=jnp.pad(b_r, (0, AUX_W - N_EXP)))
    y = moe(hdn, logits[:, :N_EXP], w1, b1, w2, b2)
    ga = jnp.repeat(mod[:, 5, :], N_CTX, axis=0, total_repeat_length=N_TOK)
    return x + ga * y


def kernel(x_prompt, x_sample, state_l0_mlstm_C, state_l0_mlstm_n, state_l0_mlstm_m, cache_l0_mla_ckv, cache_l0_mla_kpe, cache_l1_na_k, cache_l1_na_v, c, c_ctx, l0_g_mix, l0_g_moe, l0_ada_w, l0_ada_b, l0_w_in, l0_b_gate, l0_g_qa, l0_w_qb, l0_g_kva, l0_w_kvb, l0_g_qn, l0_g_kn, l0_g_hn, l0_w_out, l0_w_router, l0_b_router, l0_w1, l0_b1, l0_w2, l0_b2, l1_g_mix, l1_g_moe, l1_ada_w, l1_ada_b, l1_w_qkv, l1_g_qn, l1_g_kn, l1_rpb, l1_w_out, l1_w_router, l1_b_router, l1_w1, l1_b1, l1_w2, l1_b2):
    x = jnp.concatenate([x_prompt.reshape(N_CTX, D_MODEL), x_sample.reshape(N_LAT, D_MODEL)], axis=0)
    cond = jnp.concatenate([c_ctx[None, :], c], axis=0)

    mod0 = _adaln(cond, l0_ada_w, l0_ada_b)
    x, (new_c, new_n, new_m, new_ckv, new_kpe) = _layer0_mixer(
        x, mod0, l0_g_mix, l0_w_in, l0_b_gate, l0_g_qa, l0_w_qb, l0_g_kva, l0_w_kvb, l0_g_qn, l0_g_kn, l0_g_hn,
        l0_w_out, (state_l0_mlstm_C, state_l0_mlstm_n, state_l0_mlstm_m, cache_l0_mla_ckv, cache_l0_mla_kpe))
    x = _moe_sublayer(x, mod0, l0_g_moe, l0_w_router, l0_b_router, l0_w1, l0_b1, l0_w2, l0_b2)

    mod1 = _adaln(cond, l1_ada_w, l1_ada_b)
    x, (new_k, new_v) = _layer1_mixer(x, mod1, l1_g_mix, l1_w_qkv, l1_g_qn, l1_g_kn, l1_rpb, l1_w_out,
                                      (cache_l1_na_k, cache_l1_na_v))
    x = _moe_sublayer(x, mod1, l1_g_moe, l1_w_router, l1_b_router, l1_w1, l1_b1, l1_w2, l1_b2)

    y_prompt = x[:N_CTX].reshape(N_CTX_B, CTX_T, D_MODEL)
    y_sample = x[N_CTX:].reshape(N_LAT_B, LAT_T, D_MODEL)
    return (y_prompt, y_sample, new_c, new_n, new_m, new_ckv, new_kpe, new_k, new_v)
```

```python
import functools

import numpy as np
import jax
import jax.numpy as jnp
from jax import lax
from jax.experimental import pallas as pl
from jax.experimental.pallas import tpu as pltpu

BF = jnp.bfloat16
F32 = jnp.float32

D_MODEL = 2048
N_CTX_B, CTX_T = 16, 256
N_LAT_B, LAT_T = 4, 4096
N_CTX = N_CTX_B * CTX_T
N_LAT = N_LAT_B * LAT_T
N_TOK = N_CTX + N_LAT
PAST = 512
GRID_W = 64
GRID_ROWS = LAT_T // GRID_W
EPS = 1e-6
NEG = -1e30

NH_A, DQK_A, DV_A = 4, 128, 256
A_QK, A_V = NH_A * DQK_A, NH_A * DV_A
N_GATE = 4 * NH_A
CHUNK = 64
GATE_CAP = 15.0
NH_B, Q_LORA, KV_LORA, NOPE, ROPE, VH = 8, 512, 512, 128, 64, 128
QK_B = NOPE + ROPE
QK_B_PAD = 256
ROPE_BASE = 10000.0
NH_C, DH_C = 16, 128
WIN_R, WIN_C = 8, 16
NA_ROWS = 4
NA_KROWS = 12
NA_HEADS = 2
LOG2E = 1.4426950408889634
N_EXP, TOP_K, D_FF = 32, 4, 2048
SWIGLU_ALPHA, SWIGLU_LIMIT = 1.702, 7.0
MOE_BM = 512
MOE_TF = 1024
MOE_GRP = 256
MOE_NB = (N_TOK * TOP_K + N_EXP * (MOE_BM - 1) + MOE_BM - 1) // MOE_BM
AUX_W = 128
MLA_IN_W = 1280

VMEM_LIMIT = 56 * 1024 * 1024


def _cparams(sem):
    return pltpu.CompilerParams(dimension_semantics=sem, vmem_limit_bytes=VMEM_LIMIT)


def _split_bf16(x):
    hi = x.astype(BF)
    lo = (x - hi.astype(F32)).astype(BF)
    return hi, lo


def _pack_bf16_pairs(h):
    c = h.shape[1] // 2
    hb = h.astype(BF).astype(F32)
    hi = lax.bitcast_convert_type(hb[:, :c], jnp.uint32)
    lo = lax.bitcast_convert_type(hb[:, c:], jnp.uint32)
    return hi | (lo >> 16)


def _unpack_bf16_pairs(w):
    hi = lax.bitcast_convert_type(w & jnp.uint32(0xFFFF0000), F32).astype(BF)
    lo = lax.bitcast_convert_type(w << 16, F32).astype(BF)
    return jnp.concatenate([hi, lo], axis=1)


def _norm_mod_kernel(*refs, which, has_aux, packed):
    if has_aux:
        x_ref, g_ref, mod_ref, w_ref, b_ref, o_ref, aux_ref = refs
    else:
        x_ref, g_ref, mod_ref, o_ref = refs
    x = x_ref[...]
    ms = jnp.mean(x * x, axis=-1, keepdims=True)
    y = x * lax.rsqrt(ms + EPS) * g_ref[...]
    sh = mod_ref[0, 3 * which:3 * which + 1, :]
    sc = mod_ref[0, 3 * which + 1:3 * which + 2, :]
    h = y * (1.0 + sc) + sh
    o_ref[...] = _pack_bf16_pairs(h) if packed else h.astype(BF)
    if has_aux:
        h_hi, h_lo = _split_bf16(h)
        w_hi, w_lo = _split_bf16(w_ref[...])
        acc = jnp.dot(h_hi, w_hi, preferred_element_type=F32)
        acc += jnp.dot(h_hi, w_lo, preferred_element_type=F32)
        acc += jnp.dot(h_lo, w_hi, preferred_element_type=F32)
        aux_ref[...] = acc + b_ref[...]


def norm_mod(x, g, mod, which, aux_w=None, aux_b=None, packed=False, tm=512):
    n, d = x.shape
    has_aux = aux_w is not None
    in_specs = [
        pl.BlockSpec((tm, d), lambda i: (i, 0)),
        pl.BlockSpec((1, d), lambda i: (0, 0)),
        pl.BlockSpec((1, 6, d), lambda i: ((i * tm) // N_CTX, 0, 0)),
    ]
    args = [x, g.reshape(1, d), mod]
    od, odt = (d // 2, jnp.uint32) if packed else (d, BF)
    out_shape = [jax.ShapeDtypeStruct((n, od), odt)]
    out_specs = [pl.BlockSpec((tm, od), lambda i: (i, 0))]
    if has_aux:
        in_specs += [pl.BlockSpec((d, AUX_W), lambda i: (0, 0)), pl.BlockSpec((1, AUX_W), lambda i: (0, 0))]
        args += [aux_w, aux_b.reshape(1, AUX_W)]
        out_shape.append(jax.ShapeDtypeStruct((n, AUX_W), F32))
        out_specs.append(pl.BlockSpec((tm, AUX_W), lambda i: (i, 0)))
    res = pl.pallas_call(
        functools.partial(_norm_mod_kernel, which=which, has_aux=has_aux, packed=packed),
        grid=(n // tm,), in_specs=in_specs, out_specs=out_specs, out_shape=out_shape,
        compiler_params=_cparams(("parallel",)), name="norm_mod",
    )(*args)
    return res if has_aux else res[0]


def _group_rmsnorm(acc, gain, group, div):
    parts = []
    for c in range(acc.shape[1] // group):
        blk = acc[:, c * group:(c + 1) * group]
        ms = jnp.sum(blk * blk, axis=-1, keepdims=True) * (1.0 / div)
        parts.append(blk * lax.rsqrt(ms + EPS) * gain[:, c * group:(c + 1) * group])
    return parts[0] if len(parts) == 1 else jnp.concatenate(parts, axis=-1)


def _mm_kernel(*refs, has_bias, gate_row, norm_group, norm_div, n_norm_tiles):
    a_ref, w_ref = refs[0], refs[1]
    pos = 2
    if has_bias:
        b_ref = refs[pos]; pos += 1
    if gate_row is not None:
        x_ref, mod_ref = refs[pos], refs[pos + 1]; pos += 2
    if norm_group:
        gain_ref = refs[pos]; pos += 1
    o_ref = refs[pos]

    acc = jnp.dot(a_ref[...].astype(BF), w_ref[...].astype(BF), preferred_element_type=F32)
    if has_bias:
        acc = acc + b_ref[...]
    if gate_row is not None:
        acc = x_ref[...] + mod_ref[0, gate_row:gate_row + 1, :] * acc
    if norm_group:
        j = pl.program_id(1)

        @pl.when(j < n_norm_tiles)
        def _():
            o_ref[...] = _group_rmsnorm(acc, gain_ref[...], norm_group, norm_div).astype(o_ref.dtype)

        @pl.when(j >= n_norm_tiles)
        def _():
            o_ref[...] = acc.astype(o_ref.dtype)
    else:
        o_ref[...] = acc.astype(o_ref.dtype)


def matmul(a, w, *, bias=None, resid=None, mod=None, gate_row=None, gain=None, norm_group=0,
           norm_div=1.0, n_norm_cols=None, out_dtype=F32, tm=1024, tn=512):
    m, k = a.shape
    n = w.shape[1]
    tm = min(tm, m)
    tn = min(tn, n)
    assert m % tm == 0 and n % tn == 0
    in_specs = [pl.BlockSpec((tm, k), lambda i, j: (i, 0)), pl.BlockSpec((k, tn), lambda i, j: (0, j))]
    args = [a, w]
    if bias is not None:
        in_specs.append(pl.BlockSpec((1, tn), lambda i, j: (0, j)))
        args.append(bias.reshape(1, n))
    if gate_row is not None:
        in_specs.append(pl.BlockSpec((tm, tn), lambda i, j: (i, j)))
        in_specs.append(pl.BlockSpec((1, 6, tn), lambda i, j: ((i * tm) // N_CTX, 0, j)))
        args += [resid, mod]
    n_norm_tiles = 0
    if norm_group:
        n_norm_cols = n if n_norm_cols is None else n_norm_cols
        assert n_norm_cols % tn == 0 and tn % norm_group == 0
        n_norm_tiles = n_norm_cols // tn
        in_specs.append(pl.BlockSpec((1, tn), lambda i, j: (0, j)))
        args.append(gain.reshape(1, n))
    return pl.pallas_call(
        functools.partial(_mm_kernel, has_bias=bias is not None, gate_row=gate_row, norm_group=norm_group,
                          norm_div=norm_div, n_norm_tiles=n_norm_tiles),
        grid=(m // tm, n // tn), in_specs=in_specs,
        out_specs=pl.BlockSpec((tm, tn), lambda i, j: (i, j)),
        out_shape=jax.ShapeDtypeStruct((m, n), out_dtype),
        compiler_params=_cparams(("parallel", "parallel")), name="matmul",
    )(*args)


ATTN_CHUNK = 512


def _attn_kernel(*refs, q_scale, n_src):
    q_ref, o_ref = refs[0], refs[-1]
    q = q_ref[0]
    if q_scale is not None:
        q = q * (q_scale * LOG2E)
    q = q.astype(BF)
    nt = (((1,), (1,)), ((), ()))
    scores, values = [], []
    for i in range(n_src):
        k_ref, v_ref = refs[1 + 2 * i], refs[2 + 2 * i]
        tk = k_ref.shape[1]
        ch = min(ATTN_CHUNK, tk)
        for c in range(tk // ch):
            k = k_ref[0, c * ch:(c + 1) * ch, :].astype(BF)
            scores.append(lax.dot_general(q, k, nt, preferred_element_type=F32))
            values.append((v_ref, c * ch, ch))
    m = functools.reduce(jnp.maximum, [jnp.max(s, axis=1, keepdims=True) for s in scores])
    l, acc = None, None
    for s, (v_ref, start, ch) in zip(scores, values):
        p = jnp.exp2(s - m)
        ps = jnp.sum(p, axis=1, keepdims=True)
        pv = jnp.dot(p.astype(BF), v_ref[0, start:start + ch, :].astype(BF), preferred_element_type=F32)
        l = ps if l is None else l + ps
        acc = pv if acc is None else acc + pv
    o_ref[0] = (acc / l).astype(o_ref.dtype)


def attention(q, q_map, kv_sources, *, nb, nh, tq_len, dk, dv, tq, q_scale):
    in_specs = [pl.BlockSpec((1, tq, dk), lambda b, h, qi: (q_map(b, h)[0], qi, q_map(b, h)[1]))]
    args = [q]
    for k, k_map, v, v_map, n_keys in kv_sources:
        in_specs.append(pl.BlockSpec((1, n_keys, dk), lambda b, h, qi, f=k_map: (f(b, h)[0], 0, f(b, h)[1])))
        in_specs.append(pl.BlockSpec((1, n_keys, dv), lambda b, h, qi, f=v_map: (f(b, h)[0], 0, f(b, h)[1])))
        args += [k, v]
    return pl.pallas_call(
        functools.partial(_attn_kernel, q_scale=q_scale, n_src=len(kv_sources)),
        grid=(nb, nh, tq_len // tq), in_specs=in_specs,
        out_specs=pl.BlockSpec((1, tq, dv), lambda b, h, qi: (b, qi, h)),
        out_shape=jax.ShapeDtypeStruct((nb, tq_len, nh * dv), BF),
        compiler_params=_cparams(("parallel", "parallel", "arbitrary")), name="attention",
    )(*args)


def _rotate(x, cos, sin_a, sin_b):
    quarter = ROPE // 4
    return x * cos + pltpu.roll(x, 128 - quarter, 1) * sin_a + pltpu.roll(x, quarter, 1) * sin_b


def _mla_q_kernel(cq_ref, gqa_ref, wq_ref, gq_ref, cos_ref, sa_ref, sb_ref, o_ref, *, out_scale):
    cq = cq_ref[...]
    cqn = cq * lax.rsqrt(jnp.mean(cq * cq, axis=-1, keepdims=True) + EPS) * gqa_ref[...]
    q = jnp.dot(cqn.astype(BF), wq_ref[...], preferred_element_type=F32)
    cos, sa, sb = cos_ref[...], sa_ref[...], sb_ref[...]
    for h in range(NH_B):
        blk = q[:, h * QK_B_PAD:(h + 1) * QK_B_PAD]
        r = lax.rsqrt(jnp.sum(blk * blk, axis=-1, keepdims=True) * (1.0 / QK_B) + EPS)
        y = blk * r * gq_ref[:, h * QK_B_PAD:(h + 1) * QK_B_PAD]
        o_ref[:, h * QK_B_PAD:h * QK_B_PAD + NOPE] = (y[:, :NOPE] * out_scale).astype(o_ref.dtype)
        o_ref[:, h * QK_B_PAD + NOPE:(h + 1) * QK_B_PAD] = (
            _rotate(y[:, NOPE:], cos, sa, sb) * out_scale).astype(o_ref.dtype)


def mla_q(proj_b, g_qa, wq, gq, tables, tm=512):
    n = proj_b.shape[0]
    width = NH_B * QK_B_PAD
    tab = pl.BlockSpec((tm, 128), lambda i: (i, 0))
    return pl.pallas_call(
        functools.partial(_mla_q_kernel, out_scale=QK_B ** -0.5 * LOG2E),
        grid=(n // tm,),
        in_specs=[pl.BlockSpec((tm, Q_LORA), lambda i: (i, 0)), pl.BlockSpec((1, Q_LORA), lambda i: (0, 0)),
                  pl.BlockSpec((Q_LORA, width), lambda i: (0, 0)), pl.BlockSpec((1, width), lambda i: (0, 0)),
                  tab, tab, tab],
        out_specs=pl.BlockSpec((tm, width), lambda i: (i, 0)),
        out_shape=jax.ShapeDtypeStruct((n, width), BF),
        compiler_params=_cparams(("parallel",)), name="mla_q",
    )(proj_b, g_qa.reshape(1, Q_LORA), wq, gq.reshape(1, width), *tables)


def _mla_kv_kernel(*refs, normalize, rotate):
    if rotate:
        ckv_ref, kpe_ref, gkva_ref, w_ref, gkn_ref, gkr_ref, cos_ref, sa_ref, sb_ref = refs[:9]
        outs = refs[9:]
    else:
        ckv_ref, kpe_ref, gkva_ref, w_ref, gkn_ref, gkr_ref = refs[:6]
        outs = refs[6:]
    k_ref, v_ref = outs[0], outs[1]
    c = ckv_ref[...]
    if normalize:
        c = c * lax.rsqrt(jnp.mean(c * c, axis=-1, keepdims=True) + EPS) * gkva_ref[...]
        outs[2][...] = c
    kv = jnp.dot(c.astype(BF), w_ref[...], preferred_element_type=F32)
    kpe = kpe_ref[...]
    ss_pe = jnp.sum(kpe * kpe, axis=-1, keepdims=True)
    for h in range(NH_B):
        kn = kv[:, h * (NOPE + VH):h * (NOPE + VH) + NOPE]
        r = lax.rsqrt((jnp.sum(kn * kn, axis=-1, keepdims=True) + ss_pe) * (1.0 / QK_B) + EPS)
        kr = kpe * r * gkr_ref[...]
        if rotate:
            kr = _rotate(kr, cos_ref[...], sa_ref[...], sb_ref[...])
        k_ref[:, h * QK_B_PAD:h * QK_B_PAD + NOPE] = (kn * r * gkn_ref[...]).astype(k_ref.dtype)
        k_ref[:, h * QK_B_PAD + NOPE:(h + 1) * QK_B_PAD] = kr.astype(k_ref.dtype)
        v_ref[:, h * VH:(h + 1) * VH] = kv[:, h * (NOPE + VH) + NOPE:(h + 1) * (NOPE + VH)].astype(v_ref.dtype)


def mla_kv(ckv_src, ckv_blk, kpe_src, kpe_blk, g_kva, w_kvb, g_kn, tables, *, normalize, tm=512):
    n = ckv_src.shape[0]
    rotate = tables is not None
    kw, vw = NH_B * QK_B_PAD, NH_B * VH
    in_specs = [pl.BlockSpec((tm, KV_LORA), lambda i: (i, ckv_blk)), pl.BlockSpec((tm, 128), lambda i: (i, kpe_blk)),
                pl.BlockSpec((1, KV_LORA), lambda i: (0, 0)), pl.BlockSpec((KV_LORA, NH_B * (NOPE + VH)), lambda i: (0, 0)),
                pl.BlockSpec((1, NOPE), lambda i: (0, 0)), pl.BlockSpec((1, 128), lambda i: (0, 0))]
    args = [ckv_src, kpe_src, g_kva.reshape(1, KV_LORA), w_kvb, g_kn[:NOPE].reshape(1, NOPE),
            jnp.pad(g_kn[NOPE:], (0, 128 - ROPE)).reshape(1, 128)]
    if rotate:
        in_specs += [pl.BlockSpec((tm, 128), lambda i: (i, 0))] * 3
        args += list(tables)
    out_shape = [jax.ShapeDtypeStruct((n, kw), BF), jax.ShapeDtypeStruct((n, vw), BF)]
    out_specs = [pl.BlockSpec((tm, kw), lambda i: (i, 0)), pl.BlockSpec((tm, vw), lambda i: (i, 0))]
    if normalize:
        out_shape.append(jax.ShapeDtypeStruct((n, KV_LORA), F32))
        out_specs.append(pl.BlockSpec((tm, KV_LORA), lambda i: (i, 0)))
    return pl.pallas_call(
        functools.partial(_mla_kv_kernel, normalize=normalize, rotate=rotate),
        grid=(n // tm,), in_specs=in_specs, out_specs=out_specs, out_shape=out_shape,
        compiler_params=_cparams(("parallel",)), name="mla_kv",
    )(*args)


def _rotary_tables():
    half, quarter = ROPE // 2, ROPE // 4
    inv = ROPE_BASE ** (-jnp.arange(quarter, dtype=F32) * 2.0 / half)
    t = jnp.arange(LAT_T)
    lane = np.arange(128)
    is_s2 = ((lane % half) >= quarter) & (lane < ROPE)
    is_s1 = ((lane % half) < quarter) & (lane < ROPE)
    pos = jnp.where((lane < half)[None, :], (t // GRID_W)[:, None], (t % GRID_W)[:, None]).astype(F32)
    ang = pos * inv[lane % quarter][None, :]
    live = (lane < ROPE)[None, :]
    cos = jnp.where(live, jnp.cos(ang), 1.0)
    sin = jnp.sin(ang)
    sin_a = jnp.where(is_s1[None, :], -sin, 0.0)
    sin_b = jnp.where(is_s2[None, :], sin, 0.0)

    def full(tab, fill):
        return jnp.concatenate([jnp.full((N_CTX, 128), fill, F32), jnp.tile(tab, (N_LAT_B, 1))], axis=0)

    return full(cos, 1.0), full(sin_a, 0.0), full(sin_b, 0.0)


def _natten_kernel(q_ref, k_ref, v_ref, kc_ref, vc_ref, bias_ref, o_ref, *, scale):
    rb = pl.program_id(2)
    ws = jnp.clip(rb * NA_ROWS - WIN_R // 2, 0, GRID_ROWS - NA_KROWS)
    start = pl.multiple_of(ws * GRID_W, GRID_W)
    nkeys = NA_KROWS * GRID_W
    nt = (((1,), (1,)), ((), ()))
    for h in range(NA_HEADS):
        cs = slice(h * DH_C, (h + 1) * DH_C)
        q = (q_ref[0, :, cs] * (scale * LOG2E)).astype(BF)
        kw = k_ref[0, pl.ds(start, nkeys), cs].astype(BF)
        vw = v_ref[0, pl.ds(start, nkeys), cs].astype(BF)
        s_w = lax.dot_general(q, kw, nt, preferred_element_type=F32) + bias_ref[h, 0]
        s_c = lax.dot_general(q, kc_ref[0, :, cs].astype(BF), nt, preferred_element_type=F32)
        m = jnp.maximum(jnp.max(s_w, axis=1, keepdims=True), jnp.max(s_c, axis=1, keepdims=True))
        p_w = jnp.exp2(s_w - m)
        p_c = jnp.exp2(s_c - m)
        l = jnp.sum(p_w, axis=1, keepdims=True) + jnp.sum(p_c, axis=1, keepdims=True)
        o = jnp.dot(p_w.astype(BF), vw, preferred_element_type=F32)
        o += jnp.dot(p_c.astype(BF), vc_ref[0, :, cs].astype(BF), preferred_element_type=F32)
        o_ref[0, :, cs] = (o / l).astype(o_ref.dtype)


def _natten_bias(rpb):
    n_dr, n_dc = 2 * WIN_R - 1, 2 * WIN_C - 1
    cq = np.arange(GRID_W)[:, None]
    kc = np.arange(GRID_W)[None, :]
    cs = np.clip(cq - WIN_C // 2, 0, GRID_W - WIN_C)
    col_ok = (kc >= cs) & (kc < cs + WIN_C)
    dc = np.clip(kc - cq + WIN_C - 1, 0, n_dc - 1)
    col_sel = (dc[:, :, None] == np.arange(n_dc)).astype(np.float32)
    row_sel, row_ok = [], []
    for r0 in (0, 2 * NA_ROWS, GRID_ROWS - NA_ROWS):
        ws = int(np.clip(r0 - WIN_R // 2, 0, GRID_ROWS - NA_KROWS))
        r = r0 + np.arange(NA_ROWS)[:, None]
        kr = ws + np.arange(NA_KROWS)[None, :]
        rs = np.clip(r - WIN_R // 2, 0, GRID_ROWS - WIN_R)
        row_ok.append((kr >= rs) & (kr < rs + WIN_R))
        dr = np.clip(kr - r + WIN_R - 1, 0, n_dr - 1)
        row_sel.append((dr[:, :, None] == np.arange(n_dr)).astype(np.float32))
    row_sel, row_ok = np.stack(row_sel), np.stack(row_ok)
    hi = lax.Precision.HIGHEST
    t = jnp.einsum("hrc,sijr->hsijc", rpb.astype(F32), row_sel, precision=hi)
    t = jnp.einsum("hsijc,qkc->hsiqjk", t, col_sel, precision=hi)
    ok = row_ok[:, :, None, :, None] & col_ok[None, None, :, None, :]
    t = jnp.where(ok[None], t * LOG2E, NEG)
    return t.reshape(NH_C, 3, NA_ROWS * GRID_W, NA_KROWS * GRID_W)


def natten(qkv, k_ctx, v_ctx, rpb):
    bias = _natten_bias(rpb)
    nrb = GRID_ROWS // NA_ROWS
    tq = NA_ROWS * GRID_W
    nkeys = NA_KROWS * GRID_W

    def case(rb):
        return jnp.where(rb == 0, 0, jnp.where(rb == nrb - 1, 2, 1))

    ng = NH_C // NA_HEADS
    wd = NA_HEADS * DH_C
    return pl.pallas_call(
        functools.partial(_natten_kernel, scale=DH_C ** -0.5),
        grid=(N_LAT_B, ng, nrb),
        in_specs=[
            pl.BlockSpec((1, tq, wd), lambda b, h, rb: (1 + b, rb, h)),
            pl.BlockSpec((1, LAT_T, wd), lambda b, h, rb: (1 + b, 0, ng + h)),
            pl.BlockSpec((1, LAT_T, wd), lambda b, h, rb: (1 + b, 0, 2 * ng + h)),
            pl.BlockSpec((1, PAST, wd), lambda b, h, rb: (b, 0, h)),
            pl.BlockSpec((1, PAST, wd), lambda b, h, rb: (b, 0, h)),
            pl.BlockSpec((NA_HEADS, 1, tq, nkeys), lambda b, h, rb: (h, case(rb), 0, 0)),
        ],
        out_specs=pl.BlockSpec((1, tq, wd), lambda b, h, rb: (b, rb, h)),
        out_shape=jax.ShapeDtypeStruct((N_LAT_B, LAT_T, NH_C * DH_C), BF),
        compiler_params=_cparams(("parallel", "parallel", "arbitrary")), name="natten",
    )(qkv, qkv, qkv, k_ctx, v_ctx, bias)


def _mlstm_kernel(q_ref, k_ref, v_ref, og_ref, gates_ref, c0_ref, nm0_ref, ghn_ref,
                  y_ref, cn_ref, nmn_ref, hf_sc, hb_sc, c_sc, *, nc):
    L = CHUNK
    row = lax.broadcasted_iota(jnp.int32, (L, L), 0)
    col = lax.broadcasted_iota(jnp.int32, (L, L), 1)
    eye = row == col
    nt = (((1,), (1,)), ((), ()))
    tn = (((0,), (0,)), ((), ()))

    def to_col(r):
        return jnp.sum(jnp.where(eye, jnp.broadcast_to(r, (L, L)), 0.0), axis=1, keepdims=True)

    def chunk(c, d, n, m, reverse):
        sl = pl.ds(pl.multiple_of(c * L, L), L)
        g = gates_ref[0, 0, c]
        ic_row = g[2 * d:2 * d + 1, :]
        lf_row = g[2 * d + 1:2 * d + 2, :]
        q = q_ref[0, sl, :]
        k = k_ref[0, sl, :] * (DQK_A ** -0.5)
        v = v_ref[0, sl, :].astype(BF)
        mask = (col >= row) if reverse else (col <= row)
        mask_t = (row >= col) if reverse else (row <= col)
        lf_b = jnp.broadcast_to(lf_row, (L, L))
        b_col = jnp.sum(jnp.where(mask, lf_b, 0.0), axis=1, keepdims=True)
        lf_col = jnp.sum(jnp.where(eye, lf_b, 0.0), axis=1, keepdims=True)
        b_row = jnp.sum(jnp.where(mask_t, lf_col, 0.0), axis=0, keepdims=True)
        ic_col = to_col(ic_row)
        dmat = jnp.where(mask, b_col - b_row + ic_row, NEG)
        inter = b_col + m
        m_t = jnp.maximum(inter, jnp.max(dmat, axis=1, keepdims=True))
        a = jnp.exp(inter - m_t)
        qb = q.astype(BF)
        s = lax.dot_general(qb, k.astype(BF), nt, preferred_element_type=F32) * jnp.exp(dmat - m_t)
        c_old = c_sc[d]
        num = a * jnp.dot(qb, c_old.astype(BF), preferred_element_type=F32)
        num += jnp.dot(s.astype(BF), v, preferred_element_type=F32)
        den = a * jnp.sum(q * n, axis=1, keepdims=True) + jnp.sum(s, axis=1, keepdims=True)
        h = num / jnp.maximum(jnp.abs(den), jnp.exp(-m_t))
        b_end = jnp.sum(lf_row, axis=1, keepdims=True)
        g_col = b_end - b_col + ic_col
        m_new = jnp.maximum(b_end + m, jnp.max(g_col, axis=0, keepdims=True))
        a_s = jnp.exp(b_end + m - m_new)
        kw = k * jnp.exp(g_col - m_new)
        c_sc[d] = a_s * c_old + lax.dot_general(kw.astype(BF), v, tn, preferred_element_type=F32)
        n_new = a_s * n + jnp.sum(kw, axis=0, keepdims=True)
        return h, sl, n_new, m_new

    c_sc[0] = c0_ref[0, 0, 0]
    c_sc[1] = c0_ref[0, 1, 0]
    init = (nm0_ref[0, 0, 0, 0:1, :], nm0_ref[0, 0, 0, 1:2, 0:1],
            nm0_ref[0, 1, 0, 0:1, :], nm0_ref[0, 1, 0, 1:2, 0:1])

    def body(i, carry):
        n_f, m_f, n_b, m_b = carry
        h, sl, n_f, m_f = chunk(i, 0, n_f, m_f, False)
        hf_sc[sl, :] = h
        h, sl, n_b, m_b = chunk(nc - 1 - i, 1, n_b, m_b, True)
        hb_sc[sl, :] = h
        return n_f, m_f, n_b, m_b

    n_f, m_f, n_b, m_b = lax.fori_loop(0, nc, body, init, unroll=2)

    cn_ref[0, 0, 0] = c_sc[0]
    cn_ref[0, 1, 0] = c_sc[1]
    for d, (n_d, m_d) in enumerate(((n_f, m_f), (n_b, m_b))):
        nmn_ref[0, d, 0] = jnp.concatenate(
            [n_d, jnp.broadcast_to(m_d, (1, DQK_A)), jnp.zeros((6, DQK_A), F32)], axis=0)

    rows = 256
    gain = ghn_ref[0]

    def out_body(i, _):
        sl = pl.ds(pl.multiple_of(i * rows, rows), rows)
        hs = hf_sc[sl, :] + hb_sc[sl, :]
        hn = hs * lax.rsqrt(jnp.mean(hs * hs, axis=-1, keepdims=True) + EPS) * gain
        gate = 1.0 / (1.0 + jnp.exp(-og_ref[0, sl, :]))
        y_ref[0, sl, :] = (gate * hn).astype(y_ref.dtype)
        return 0

    lax.fori_loop(0, (nc * L) // rows, out_body, 0)


def mlstm(proj, gates, c0, nm0, g_hn, *, nb, t_len, boff):
    nc = t_len // CHUNK
    kq = A_QK // DQK_A
    kv = 2 * A_QK // DV_A
    ko = (2 * A_QK + A_V) // DV_A
    return pl.pallas_call(
        functools.partial(_mlstm_kernel, nc=nc),
        grid=(nb, NH_A),
        in_specs=[
            pl.BlockSpec((1, t_len, DQK_A), lambda b, h: (boff + b, 0, h)),
            pl.BlockSpec((1, t_len, DQK_A), lambda b, h: (boff + b, 0, kq + h)),
            pl.BlockSpec((1, t_len, DV_A), lambda b, h: (boff + b, 0, kv + h)),
            pl.BlockSpec((1, t_len, DV_A), lambda b, h: (boff + b, 0, ko + h)),
            pl.BlockSpec((1, 1, nc, 4, CHUNK), lambda b, h: (b, h, 0, 0, 0)),
            pl.BlockSpec((1, 2, 1, DQK_A, DV_A), lambda b, h: (b, 0, h, 0, 0)),
            pl.BlockSpec((1, 2, 1, 8, DQK_A), lambda b, h: (b, 0, h, 0, 0)),
            pl.BlockSpec((1, 1, DV_A), lambda b, h: (h, 0, 0)),
        ],
        out_specs=[
            pl.BlockSpec((1, t_len, DV_A), lambda b, h: (b, 0, h)),
            pl.BlockSpec((1, 2, 1, DQK_A, DV_A), lambda b, h: (b, 0, h, 0, 0)),
            pl.BlockSpec((1, 2, 1, 8, DQK_A), lambda b, h: (b, 0, h, 0, 0)),
        ],
        out_shape=[
            jax.ShapeDtypeStruct((nb, t_len, A_V), BF),
            jax.ShapeDtypeStruct((nb, 2, NH_A, DQK_A, DV_A), F32),
            jax.ShapeDtypeStruct((nb, 2, NH_A, 8, DQK_A), F32),
        ],
        scratch_shapes=[pltpu.VMEM((t_len, DV_A), F32), pltpu.VMEM((t_len, DV_A), F32),
                        pltpu.VMEM((2, DQK_A, DV_A), F32)],
        compiler_params=_cparams(("parallel", "parallel")), name="mlstm",
    )(proj, proj, proj, proj, gates, c0, nm0, g_hn.reshape(NH_A, 1, DV_A))


def _w1_prep_kernel(w_ref, p_ref, o_ref):
    gw = 2 * MOE_GRP
    for c in range(w_ref.shape[2] // gw):
        blk = w_ref[0, :, c * gw:(c + 1) * gw].astype(BF)
        o_ref[0, :, c * gw:(c + 1) * gw] = jnp.dot(blk, p_ref[...], preferred_element_type=F32).astype(BF)


def _deinterleave_perm():
    gw = 2 * MOE_GRP
    j = np.arange(gw)
    src = np.where(j < MOE_GRP, 2 * j, 2 * (j - MOE_GRP) + 1)
    perm = np.zeros((gw, gw), np.float32)
    perm[src, j] = 1.0
    return jnp.asarray(perm, BF)


def w1_prep(w1):
    ne, d, n2 = w1.shape
    tn = 1024
    gw = 2 * MOE_GRP
    return pl.pallas_call(
        _w1_prep_kernel,
        grid=(ne, n2 // tn),
        in_specs=[pl.BlockSpec((1, d, tn), lambda e, j: (e, 0, j)), pl.BlockSpec((gw, gw), lambda e, j: (0, 0))],
        out_specs=pl.BlockSpec((1, d, tn), lambda e, j: (e, 0, j)),
        out_shape=jax.ShapeDtypeStruct((ne, d, n2), BF),
        compiler_params=_cparams(("parallel", "parallel")), name="w1_prep",
    )(w1, _deinterleave_perm())


def _cast_kernel(x_ref, o_ref):
    o_ref[...] = x_ref[...].astype(o_ref.dtype)


def cast_bf16(x, rows):
    ne, r, c = x.shape
    return pl.pallas_call(
        _cast_kernel,
        grid=(ne, r // rows),
        in_specs=[pl.BlockSpec((1, rows, c), lambda e, j: (e, j, 0))],
        out_specs=pl.BlockSpec((1, rows, c), lambda e, j: (e, j, 0)),
        out_shape=jax.ShapeDtypeStruct(x.shape, BF),
        compiler_params=_cparams(("parallel", "parallel")), name="cast_bf16",
    )(x)


def _moe_kernel(blk_e_ref, nused_ref, x_ref, w1_ref, b1_ref, w2_ref, b2_ref, o_ref):
    b = pl.program_id(0)
    f = pl.program_id(1)

    @pl.when(f == 0)
    def _():
        o_ref[...] = jnp.broadcast_to(b2_ref[0], o_ref.shape)

    @pl.when(b < nused_ref[0])
    def _():
        x = _unpack_bf16_pairs(x_ref[...])
        gw = 2 * MOE_GRP
        part = None
        for c in range(MOE_TF // MOE_GRP):
            hh = jnp.dot(x, w1_ref[0, :, c * gw:(c + 1) * gw], preferred_element_type=F32)
            hh = hh + b1_ref[0, :, c * gw:(c + 1) * gw]
            glu = jnp.minimum(hh[:, :MOE_GRP], SWIGLU_LIMIT)
            lin = jnp.clip(hh[:, MOE_GRP:], -SWIGLU_LIMIT, SWIGLU_LIMIT)
            act = glu * (1.0 / (1.0 + jnp.exp(-SWIGLU_ALPHA * glu))) * (lin + 1.0)
            p = jnp.dot(act.astype(BF), w2_ref[0, c * MOE_GRP:(c + 1) * MOE_GRP, :], preferred_element_type=F32)
            part = p if part is None else part + p
        o_ref[...] += part


def moe_experts(xb, blk_e, nused, w1p, b1p, w2, b2):
    nf = D_FF // MOE_TF
    d = D_MODEL

    def eidx(b, be):
        return be[b]

    def fidx(b, f, nu):
        return jnp.where(b < nu[0], f, nf - 1)

    grid_spec = pltpu.PrefetchScalarGridSpec(
        num_scalar_prefetch=2,
        grid=(MOE_NB, nf),
        in_specs=[
            pl.BlockSpec((MOE_BM, d // 2), lambda b, f, be, nu: (jnp.minimum(b, nu[0] - 1), 0)),
            pl.BlockSpec((1, d, 2 * MOE_TF), lambda b, f, be, nu: (eidx(b, be), 0, fidx(b, f, nu))),
            pl.BlockSpec((1, 1, 2 * MOE_TF), lambda b, f, be, nu: (eidx(b, be), 0, fidx(b, f, nu))),
            pl.BlockSpec((1, MOE_TF, d), lambda b, f, be, nu: (eidx(b, be), fidx(b, f, nu), 0)),
            pl.BlockSpec((1, 1, d), lambda b, f, be, nu: (eidx(b, be), 0, 0)),
        ],
        out_specs=pl.BlockSpec((MOE_BM, d), lambda b, f, be, nu: (b, 0)),
    )
    return pl.pallas_call(
        _moe_kernel,
        grid_spec=grid_spec,
        out_shape=jax.ShapeDtypeStruct((MOE_NB * MOE_BM, d), F32),
        compiler_params=_cparams(("arbitrary", "arbitrary")), name="moe_experts",
    )(blk_e, nused, xb, w1p, b1p, w2, b2)


def moe(hdn, logits, w1, b1, w2, b2):
    n = hdn.shape[0]
    nk = n * TOP_K
    top_v, top_i = lax.top_k(logits, TOP_K)
    gate = jax.nn.softmax(top_v, axis=-1)
    e = top_i.reshape(-1).astype(jnp.int32)
    rb = 512
    onehot = (e[None, :] == jnp.arange(N_EXP, dtype=jnp.int32)[:, None]).astype(BF).reshape(N_EXP, nk // rb, rb)
    within = jnp.einsum("ebs,ts->ebt", onehot, jnp.tril(jnp.ones((rb, rb), BF)), preferred_element_type=F32)
    tot = within[:, :, -1]
    before = jnp.cumsum(tot, axis=1) - tot
    rank = jnp.sum((within + before[:, :, None]) * onehot.astype(F32), axis=0).reshape(nk).astype(jnp.int32) - 1
    counts = (before[:, -1] + tot[:, -1]).astype(jnp.int32)
    padded = (counts + MOE_BM - 1) // MOE_BM * MOE_BM
    pad_end = jnp.cumsum(padded)
    pad_start = pad_end - padded
    dest = pad_start[e] + rank
    tok = jnp.arange(nk, dtype=jnp.int32) // TOP_K
    slot_tok = jnp.zeros((MOE_NB * MOE_BM,), jnp.int32).at[dest].set(tok)
    blk_start = jnp.arange(MOE_NB, dtype=jnp.int32) * MOE_BM
    blk_e = jnp.minimum(jnp.sum((pad_end[None, :] <= blk_start[:, None]).astype(jnp.int32), axis=1), N_EXP - 1)
    nused = (pad_end[-1] // MOE_BM).astype(jnp.int32).reshape(1)
    xb = jnp.take(hdn, slot_tok, axis=0)
    b1p = b1.reshape(N_EXP, D_FF // MOE_GRP, MOE_GRP, 2).transpose(0, 1, 3, 2).reshape(N_EXP, 1, 2 * D_FF)
    yb = moe_experts(xb, blk_e, nused, w1_prep(w1), b1p, cast_bf16(w2, MOE_TF), b2[:, None, :])
    return yb, dest.reshape(n, TOP_K), gate


def moe_combine(yb, dest, gate, lo, hi):
    y = None
    for k in range(TOP_K):
        yk = jnp.take(yb, dest[lo:hi, k], axis=0) * gate[lo:hi, k:k + 1]
        y = yk if y is None else y + yk
    return y


def _pad_cols(w, width):
    return jnp.pad(w, ((0, 0), (0, width - w.shape[1])))


def _adaln(cond, ada_w, ada_b):
    a = jnp.pad(jax.nn.silu(cond), ((0, 8 - cond.shape[0]), (0, 0)))
    mod = matmul(a, ada_w, bias=ada_b, tm=8, tn=1024)
    return mod[:cond.shape[0]].reshape(cond.shape[0], 6, D_MODEL)


def _layer0_mixer(x, mod, g_mix, w_in, b_gate, g_qa, w_qb, g_kva, w_kvb, g_qn, g_kn, g_hn, w_out, cache):
    st_c, st_n, st_m, c_ckv, c_kpe = cache
    o_gate = 2 * A_QK + 2 * A_V
    o_cq = o_gate + N_GATE
    w_gate = _pad_cols(w_in[:, o_gate:o_cq], AUX_W)
    hdn, gates = norm_mod(x, g_mix, mod, 0, aux_w=w_gate, aux_b=jnp.pad(b_gate, (0, AUX_W - N_GATE)))
    proj = matmul(hdn, w_in[:, :o_gate].astype(BF))
    w_b = _pad_cols(w_in[:, o_cq:], MLA_IN_W).astype(BF)
    proj_b = matmul(hdn, w_b, tn=MLA_IN_W)

    gt = GATE_CAP * jnp.tanh(gates[:, :N_GATE] / GATE_CAP)
    gt = gt.reshape(N_TOK, 4, NH_A)
    gt = jnp.stack([gt[:, 0], jax.nn.log_sigmoid(gt[:, 1]), gt[:, 2], jax.nn.log_sigmoid(gt[:, 3])], axis=1)

    def gate_layout(g, nb, t_len):
        g = g.reshape(nb, t_len // CHUNK, CHUNK, 4, NH_A)
        return g.transpose(0, 4, 1, 3, 2)

    zc = jnp.zeros((N_CTX_B, 2, NH_A, DQK_A, DV_A), F32)
    znm = jnp.zeros((N_CTX_B, 2, NH_A, 8, DQK_A), F32)
    y_ctx, new_c, new_nm = mlstm(proj.reshape(N_TOK // CTX_T, CTX_T, -1), gate_layout(gt[:N_CTX], N_CTX_B, CTX_T),
                                 zc, znm, g_hn, nb=N_CTX_B, t_len=CTX_T, boff=0)
    nm_lat = jnp.concatenate([st_n[:, :, :, None, :],
                              jnp.broadcast_to(st_m[:, :, :, None, None], (N_LAT_B, 2, NH_A, 1, DQK_A)),
                              jnp.zeros((N_LAT_B, 2, NH_A, 6, DQK_A), F32)], axis=3)
    y_lat, _, _ = mlstm(proj.reshape(N_TOK // LAT_T, LAT_T, -1), gate_layout(gt[N_CTX:], N_LAT_B, LAT_T),
                        st_c, nm_lat, g_hn, nb=N_LAT_B, t_len=LAT_T, boff=1)
    y_a = jnp.concatenate([y_ctx.reshape(N_CTX, A_V), y_lat.reshape(N_LAT, A_V)], axis=0)
    new_n = new_nm[:, :, :, 0, :]
    new_m = new_nm[:, :, :, 1, 0]

    wq = w_qb.reshape(Q_LORA, NH_B, QK_B)
    wq = jnp.pad(wq, ((0, 0), (0, 0), (0, QK_B_PAD - QK_B))).reshape(Q_LORA, NH_B * QK_B_PAD).astype(BF)
    gq = jnp.tile(jnp.pad(g_qn, (0, QK_B_PAD - QK_B)), NH_B)
    tables = _rotary_tables()
    w_kvb_b = w_kvb.astype(BF)
    q = mla_q(proj_b, g_qa, wq, gq, tables)
    k_new, v_new, ckv_n = mla_kv(proj_b, 1, proj_b, (Q_LORA + KV_LORA) // 128, g_kva, w_kvb_b, g_kn, tables,
                                 normalize=True)
    k_old, v_old = mla_kv(c_ckv.reshape(N_LAT_B * PAST, KV_LORA), 0,
                          _pad_cols(c_kpe.reshape(N_LAT_B * PAST, ROPE), 128), 0, g_kva, w_kvb_b, g_kn, None,
                          normalize=False)
    kpe = proj_b[:N_CTX, Q_LORA + KV_LORA:Q_LORA + KV_LORA + ROPE]
    ident = lambda b, h: (b, h)
    lat = lambda b, h: (1 + b, h)
    kw, vw = NH_B * QK_B_PAD, NH_B * VH
    y_b_ctx = attention(
        q.reshape(N_TOK // CTX_T, CTX_T, kw), ident,
        [(k_new.reshape(N_TOK // CTX_T, CTX_T, kw), ident, v_new.reshape(N_TOK // CTX_T, CTX_T, vw), ident, CTX_T)],
        nb=N_CTX_B, nh=NH_B, tq_len=CTX_T, dk=QK_B_PAD, dv=VH, tq=CTX_T, q_scale=None)
    y_b_lat = attention(
        q.reshape(N_TOK // LAT_T, LAT_T, kw), lat,
        [(k_new.reshape(N_TOK // LAT_T, LAT_T, kw), lat, v_new.reshape(N_TOK // LAT_T, LAT_T, vw), lat, LAT_T),
         (k_old.reshape(N_LAT_B, PAST, kw), ident, v_old.reshape(N_LAT_B, PAST, vw), ident, PAST)],
        nb=N_LAT_B, nh=NH_B, tq_len=LAT_T, dk=QK_B_PAD, dv=VH, tq=256, q_scale=None)
    y_b = jnp.concatenate([y_b_ctx.reshape(N_CTX, -1), y_b_lat.reshape(N_LAT, -1)], axis=0)

    y = jnp.concatenate([y_a, y_b], axis=1)
    x = matmul(y, w_out.astype(BF), resid=x, mod=mod, gate_row=2)
    new = (new_c, new_n, new_m, ckv_n[:N_CTX].reshape(N_CTX_B, CTX_T, KV_LORA), kpe.reshape(N_CTX_B, CTX_T, ROPE))
    return x, new


def _layer1_mixer(x, mod, g_mix, w_qkv, g_qn, g_kn, rpb, w_out, cache):
    c_k, c_v = cache
    hd = NH_C * DH_C
    hdn = norm_mod(x, g_mix, mod, 0)
    gain = jnp.concatenate([jnp.tile(g_qn, NH_C), jnp.tile(g_kn, NH_C), jnp.ones((hd,), F32)])
    qkv = matmul(hdn, w_qkv.astype(BF), gain=gain, norm_group=DH_C, norm_div=float(DH_C), n_norm_cols=2 * hd)
    new_k = qkv[:N_CTX, hd:2 * hd].reshape(N_CTX_B, CTX_T, NH_C, DH_C)
    new_v = qkv[:N_CTX, 2 * hd:].reshape(N_CTX_B, CTX_T, NH_C, DH_C)
    qkv_c = qkv.reshape(N_TOK // CTX_T, CTX_T, 3 * hd)
    o_ctx = attention(
        qkv_c, lambda b, h: (b, h),
        [(qkv_c, lambda b, h: (b, NH_C + h), qkv_c, lambda b, h: (b, 2 * NH_C + h), CTX_T)],
        nb=N_CTX_B, nh=NH_C, tq_len=CTX_T, dk=DH_C, dv=DH_C, tq=CTX_T, q_scale=DH_C ** -0.5)
    o_lat = natten(qkv.reshape(N_TOK // LAT_T, LAT_T, 3 * hd), c_k.reshape(N_LAT_B, PAST, hd),
                   c_v.reshape(N_LAT_B, PAST, hd), rpb)
    o = jnp.concatenate([o_ctx.reshape(N_CTX, hd), o_lat.reshape(N_LAT, hd)], axis=0)
    x = matmul(o, w_out.astype(BF), resid=x, mod=mod, gate_row=2)
    return x, (new_k, new_v)


def _moe_sublayer(x, mod, g_moe, w_r, b_r, w1, b1, w2, b2, split=False):
    hdn, logits = norm_mod(x, g_moe, mod, 1, aux_w=_pad_cols(w_r, AUX_W), aux_b=jnp.pad(b_r, (0, AUX_W - N_EXP)),
                           packed=True)
    yb, dest, gate = moe(hdn, logits[:, :N_EXP], w1, b1, w2, b2)
    ga = mod[:, 5, :]
    if not split:
        ga_rows = jnp.repeat(ga, N_CTX, axis=0, total_repeat_length=N_TOK)
        return x + ga_rows * moe_combine(yb, dest, gate, 0, N_TOK)
    y_ctx = x[:N_CTX] + ga[0][None, :] * moe_combine(yb, dest, gate, 0, N_CTX)
    y_lat = x[N_CTX:].reshape(N_LAT_B, LAT_T, D_MODEL) + ga[1:, None, :] * moe_combine(
        yb, dest, gate, N_CTX, N_TOK).reshape(N_LAT_B, LAT_T, D_MODEL)
    return y_ctx, y_lat.reshape(N_LAT, D_MODEL)


def kernel(x_prompt, x_sample, state_l0_mlstm_C, state_l0_mlstm_n, state_l0_mlstm_m, cache_l0_mla_ckv, cache_l0_mla_kpe, cache_l1_na_k, cache_l1_na_v, c, c_ctx, l0_g_mix, l0_g_moe, l0_ada_w, l0_ada_b, l0_w_in, l0_b_gate, l0_g_qa, l0_w_qb, l0_g_kva, l0_w_kvb, l0_g_qn, l0_g_kn, l0_g_hn, l0_w_out, l0_w_router, l0_b_router, l0_w1, l0_b1, l0_w2, l0_b2, l1_g_mix, l1_g_moe, l1_ada_w, l1_ada_b, l1_w_qkv, l1_g_qn, l1_g_kn, l1_rpb, l1_w_out, l1_w_router, l1_b_router, l1_w1, l1_b1, l1_w2, l1_b2):
    x = jnp.concatenate([x_prompt.reshape(N_CTX, D_MODEL), x_sample.reshape(N_LAT, D_MODEL)], axis=0)
    cond = jnp.concatenate([c_ctx[None, :], c], axis=0)

    mod0 = _adaln(cond, l0_ada_w, l0_ada_b)
    x, (new_c, new_n, new_m, new_ckv, new_kpe) = _layer0_mixer(
        x, mod0, l0_g_mix, l0_w_in, l0_b_gate, l0_g_qa, l0_w_qb, l0_g_kva, l0_w_kvb, l0_g_qn, l0_g_kn, l0_g_hn,
        l0_w_out, (state_l0_mlstm_C, state_l0_mlstm_n, state_l0_mlstm_m, cache_l0_mla_ckv, cache_l0_mla_kpe))
    x = _moe_sublayer(x, mod0, l0_g_moe, l0_w_router, l0_b_router, l0_w1, l0_b1, l0_w2, l0_b2)

    mod1 = _adaln(cond, l1_ada_w, l1_ada_b)
    x, (new_k, new_v) = _layer1_mixer(x, mod1, l1_g_mix, l1_w_qkv, l1_g_qn, l1_g_kn, l1_rpb, l1_w_out,
                                      (cache_l1_na_k, cache_l1_na_v))
    y_ctx, y_lat = _moe_sublayer(x, mod1, l1_g_moe, l1_w_router, l1_b_router, l1_w1, l1_b1, l1_w2, l1_b2, split=True)

    y_prompt = y_ctx.reshape(N_CTX_B, CTX_T, D_MODEL)
    y_sample = y_lat.reshape(N_LAT_B, LAT_T, D_MODEL)
    return (y_prompt, y_sample, new_c, new_n, new_m, new_ckv, new_kpe, new_k, new_v)
```

```python
import functools

import numpy as np
import jax
import jax.numpy as jnp
from jax import lax
from jax.experimental import pallas as pl
from jax.experimental.pallas import tpu as pltpu

BF = jnp.bfloat16
F32 = jnp.float32

D_MODEL = 2048
N_CTX_B, CTX_T = 16, 256
N_LAT_B, LAT_T = 4, 4096
N_CTX = N_CTX_B * CTX_T
N_LAT = N_LAT_B * LAT_T
N_TOK = N_CTX + N_LAT
PAST = 512
GRID_W = 64
GRID_ROWS = LAT_T // GRID_W
EPS = 1e-6
NEG = -1e30

NH_A, DQK_A, DV_A = 4, 128, 256
A_QK, A_V = NH_A * DQK_A, NH_A * DV_A
N_GATE = 4 * NH_A
CHUNK = 64
GATE_CAP = 15.0
NH_B, Q_LORA, KV_LORA, NOPE, ROPE, VH = 8, 512, 512, 128, 64, 128
QK_B = NOPE + ROPE
QK_B_PAD = 256
ROPE_BASE = 10000.0
NH_C, DH_C = 16, 128
WIN_R, WIN_C = 8, 16
NA_ROWS = 4
NA_KROWS = 12
NA_HEADS = 2
LOG2E = 1.4426950408889634
N_EXP, TOP_K, D_FF = 32, 4, 2048
SWIGLU_ALPHA, SWIGLU_LIMIT = 1.702, 7.0
MOE_BM = 512
MOE_TF = 1024
MOE_GRP = 256
MOE_NB = (N_TOK * TOP_K + N_EXP * (MOE_BM - 1) + MOE_BM - 1) // MOE_BM
AUX_W = 128
MLA_IN_W = 1280

VMEM_LIMIT = 56 * 1024 * 1024


def _cparams(sem):
    return pltpu.CompilerParams(dimension_semantics=sem, vmem_limit_bytes=VMEM_LIMIT)


def _split_bf16(x):
    hi = x.astype(BF)
    lo = (x - hi.astype(F32)).astype(BF)
    return hi, lo


def _pack_bf16_pairs(h):
    c = h.shape[1] // 2
    hb = h.astype(BF).astype(F32)
    hi = lax.bitcast_convert_type(hb[:, :c], jnp.uint32)
    lo = lax.bitcast_convert_type(hb[:, c:], jnp.uint32)
    return hi | (lo >> 16)


def _unpack_bf16_pairs(w):
    hi = lax.bitcast_convert_type(w & jnp.uint32(0xFFFF0000), F32).astype(BF)
    lo = lax.bitcast_convert_type(w << 16, F32).astype(BF)
    return jnp.concatenate([hi, lo], axis=1)


def _norm_mod_kernel(*refs, which, has_aux, packed):
    if has_aux:
        x_ref, g_ref, mod_ref, w_ref, b_ref, o_ref, aux_ref = refs
    else:
        x_ref, g_ref, mod_ref, o_ref = refs
    x = x_ref[...]
    ms = jnp.mean(x * x, axis=-1, keepdims=True)
    y = x * lax.rsqrt(ms + EPS) * g_ref[...]
    sh = mod_ref[0, 3 * which:3 * which + 1, :]
    sc = mod_ref[0, 3 * which + 1:3 * which + 2, :]
    h = y * (1.0 + sc) + sh
    o_ref[...] = _pack_bf16_pairs(h) if packed else h.astype(BF)
    if has_aux:
        h_hi, h_lo = _split_bf16(h)
        w_hi, w_lo = _split_bf16(w_ref[...])
        acc = jnp.dot(h_hi, w_hi, preferred_element_type=F32)
        acc += jnp.dot(h_hi, w_lo, preferred_element_type=F32)
        acc += jnp.dot(h_lo, w_hi, preferred_element_type=F32)
        aux_ref[...] = acc + b_ref[...]


def norm_mod(x, g, mod, which, aux_w=None, aux_b=None, packed=False, tm=512):
    n, d = x.shape
    has_aux = aux_w is not None
    in_specs = [
        pl.BlockSpec((tm, d), lambda i: (i, 0)),
        pl.BlockSpec((1, d), lambda i: (0, 0)),
        pl.BlockSpec((1, 6, d), lambda i: ((i * tm) // N_CTX, 0, 0)),
    ]
    args = [x, g.reshape(1, d), mod]
    od, odt = (d // 2, jnp.uint32) if packed else (d, BF)
    out_shape = [jax.ShapeDtypeStruct((n, od), odt)]
    out_specs = [pl.BlockSpec((tm, od), lambda i: (i, 0))]
    if has_aux:
        in_specs += [pl.BlockSpec((d, AUX_W), lambda i: (0, 0)), pl.BlockSpec((1, AUX_W), lambda i: (0, 0))]
        args += [aux_w, aux_b.reshape(1, AUX_W)]
        out_shape.append(jax.ShapeDtypeStruct((n, AUX_W), F32))
        out_specs.append(pl.BlockSpec((tm, AUX_W), lambda i: (i, 0)))
    res = pl.pallas_call(
        functools.partial(_norm_mod_kernel, which=which, has_aux=has_aux, packed=packed),
        grid=(n // tm,), in_specs=in_specs, out_specs=out_specs, out_shape=out_shape,
        compiler_params=_cparams(("parallel",)), name="norm_mod",
    )(*args)
    return res if has_aux else res[0]


def _group_rmsnorm(acc, gain, group, div):
    parts = []
    for c in range(acc.shape[1] // group):
        blk = acc[:, c * group:(c + 1) * group]
        ms = jnp.sum(blk * blk, axis=-1, keepdims=True) * (1.0 / div)
        parts.append(blk * lax.rsqrt(ms + EPS) * gain[:, c * group:(c + 1) * group])
    return parts[0] if len(parts) == 1 else jnp.concatenate(parts, axis=-1)


def _mm_kernel(*refs, has_bias, gate_row, norm_group, norm_div, n_norm_tiles):
    a_ref, w_ref = refs[0], refs[1]
    pos = 2
    if has_bias:
        b_ref = refs[pos]; pos += 1
    if gate_row is not None:
        x_ref, mod_ref = refs[pos], refs[pos + 1]; pos += 2
    if norm_group:
        gain_ref = refs[pos]; pos += 1
    o_ref = refs[pos]

    acc = jnp.dot(a_ref[...].astype(BF), w_ref[...].astype(BF), preferred_element_type=F32)
    if has_bias:
        acc = acc + b_ref[...]
    if gate_row is not None:
        acc = x_ref[...] + mod_ref[0, gate_row:gate_row + 1, :] * acc
    if norm_group:
        j = pl.program_id(1)

        @pl.when(j < n_norm_tiles)
        def _():
            o_ref[...] = _group_rmsnorm(acc, gain_ref[...], norm_group, norm_div).astype(o_ref.dtype)

        @pl.when(j >= n_norm_tiles)
        def _():
            o_ref[...] = acc.astype(o_ref.dtype)
    else:
        o_ref[...] = acc.astype(o_ref.dtype)


def matmul(a, w, *, bias=None, resid=None, mod=None, gate_row=None, gain=None, norm_group=0,
           norm_div=1.0, n_norm_cols=None, out_dtype=F32, tm=1024, tn=512):
    m, k = a.shape
    n = w.shape[1]
    tm = min(tm, m)
    tn = min(tn, n)
    assert m % tm == 0 and n % tn == 0
    in_specs = [pl.BlockSpec((tm, k), lambda i, j: (i, 0)), pl.BlockSpec((k, tn), lambda i, j: (0, j))]
    args = [a, w]
    if bias is not None:
        in_specs.append(pl.BlockSpec((1, tn), lambda i, j: (0, j)))
        args.append(bias.reshape(1, n))
    if gate_row is not None:
        in_specs.append(pl.BlockSpec((tm, tn), lambda i, j: (i, j)))
        in_specs.append(pl.BlockSpec((1, 6, tn), lambda i, j: ((i * tm) // N_CTX, 0, j)))
        args += [resid, mod]
    n_norm_tiles = 0
    if norm_group:
        n_norm_cols = n if n_norm_cols is None else n_norm_cols
        assert n_norm_cols % tn == 0 and tn % norm_group == 0
        n_norm_tiles = n_norm_cols // tn
        in_specs.append(pl.BlockSpec((1, tn), lambda i, j: (0, j)))
        args.append(gain.reshape(1, n))
    return pl.pallas_call(
        functools.partial(_mm_kernel, has_bias=bias is not None, gate_row=gate_row, norm_group=norm_group,
                          norm_div=norm_div, n_norm_tiles=n_norm_tiles),
        grid=(m // tm, n // tn), in_specs=in_specs,
        out_specs=pl.BlockSpec((tm, tn), lambda i, j: (i, j)),
        out_shape=jax.ShapeDtypeStruct((m, n), out_dtype),
        compiler_params=_cparams(("parallel", "parallel")), name="matmul",
    )(*args)


ATTN_CHUNK = 512


def _attn_kernel(*refs, q_scale, n_src):
    q_ref, o_ref = refs[0], refs[-1]
    q = q_ref[0]
    if q_scale is not None:
        q = q * (q_scale * LOG2E)
    q = q.astype(BF)
    nt = (((1,), (1,)), ((), ()))
    scores, values = [], []
    for i in range(n_src):
        k_ref, v_ref = refs[1 + 2 * i], refs[2 + 2 * i]
        tk = k_ref.shape[1]
        ch = min(ATTN_CHUNK, tk)
        for c in range(tk // ch):
            k = k_ref[0, c * ch:(c + 1) * ch, :].astype(BF)
            scores.append(lax.dot_general(q, k, nt, preferred_element_type=F32))
            values.append((v_ref, c * ch, ch))
    m = functools.reduce(jnp.maximum, [jnp.max(s, axis=1, keepdims=True) for s in scores])
    l, acc = None, None
    for s, (v_ref, start, ch) in zip(scores, values):
        p = jnp.exp2(s - m)
        ps = jnp.sum(p, axis=1, keepdims=True)
        pv = jnp.dot(p.astype(BF), v_ref[0, start:start + ch, :].astype(BF), preferred_element_type=F32)
        l = ps if l is None else l + ps
        acc = pv if acc is None else acc + pv
    o_ref[0] = (acc / l).astype(o_ref.dtype)


def attention(q, q_map, kv_sources, *, nb, nh, tq_len, dk, dv, tq, q_scale):
    in_specs = [pl.BlockSpec((1, tq, dk), lambda b, h, qi: (q_map(b, h)[0], qi, q_map(b, h)[1]))]
    args = [q]
    for k, k_map, v, v_map, n_keys in kv_sources:
        in_specs.append(pl.BlockSpec((1, n_keys, dk), lambda b, h, qi, f=k_map: (f(b, h)[0], 0, f(b, h)[1])))
        in_specs.append(pl.BlockSpec((1, n_keys, dv), lambda b, h, qi, f=v_map: (f(b, h)[0], 0, f(b, h)[1])))
        args += [k, v]
    return pl.pallas_call(
        functools.partial(_attn_kernel, q_scale=q_scale, n_src=len(kv_sources)),
        grid=(nb, nh, tq_len // tq), in_specs=in_specs,
        out_specs=pl.BlockSpec((1, tq, dv), lambda b, h, qi: (b, qi, h)),
        out_shape=jax.ShapeDtypeStruct((nb, tq_len, nh * dv), BF),
        compiler_params=_cparams(("parallel", "parallel", "arbitrary")), name="attention",
    )(*args)


def _rotate(x, cos, sin_a, sin_b):
    quarter = ROPE // 4
    return x * cos + pltpu.roll(x, 128 - quarter, 1) * sin_a + pltpu.roll(x, quarter, 1) * sin_b


def _mla_q_kernel(cq_ref, gqa_ref, wq_ref, gq_ref, cos_ref, sa_ref, sb_ref, o_ref, *, out_scale):
    cq = cq_ref[...]
    cqn = cq * lax.rsqrt(jnp.mean(cq * cq, axis=-1, keepdims=True) + EPS) * gqa_ref[...]
    q = jnp.dot(cqn.astype(BF), wq_ref[...], preferred_element_type=F32)
    cos, sa, sb = cos_ref[...], sa_ref[...], sb_ref[...]
    for h in range(NH_B):
        blk = q[:, h * QK_B_PAD:(h + 1) * QK_B_PAD]
        r = lax.rsqrt(jnp.sum(blk * blk, axis=-1, keepdims=True) * (1.0 / QK_B) + EPS)
        y = blk * r * gq_ref[:, h * QK_B_PAD:(h + 1) * QK_B_PAD]
        o_ref[:, h * QK_B_PAD:h * QK_B_PAD + NOPE] = (y[:, :NOPE] * out_scale).astype(o_ref.dtype)
        o_ref[:, h * QK_B_PAD + NOPE:(h + 1) * QK_B_PAD] = (
            _rotate(y[:, NOPE:], cos, sa, sb) * out_scale).astype(o_ref.dtype)


def mla_q(proj_b, g_qa, wq, gq, tables, tm=512):
    n = proj_b.shape[0]
    width = NH_B * QK_B_PAD
    tab = pl.BlockSpec((tm, 128), lambda i: (i, 0))
    return pl.pallas_call(
        functools.partial(_mla_q_kernel, out_scale=QK_B ** -0.5 * LOG2E),
        grid=(n // tm,),
        in_specs=[pl.BlockSpec((tm, Q_LORA), lambda i: (i, 0)), pl.BlockSpec((1, Q_LORA), lambda i: (0, 0)),
                  pl.BlockSpec((Q_LORA, width), lambda i: (0, 0)), pl.BlockSpec((1, width), lambda i: (0, 0)),
                  tab, tab, tab],
        out_specs=pl.BlockSpec((tm, width), lambda i: (i, 0)),
        out_shape=jax.ShapeDtypeStruct((n, width), BF),
        compiler_params=_cparams(("parallel",)), name="mla_q",
    )(proj_b, g_qa.reshape(1, Q_LORA), wq, gq.reshape(1, width), *tables)


def _mla_kv_kernel(*refs, normalize, rotate):
    if rotate:
        ckv_ref, kpe_ref, gkva_ref, w_ref, gkn_ref, gkr_ref, cos_ref, sa_ref, sb_ref = refs[:9]
        outs = refs[9:]
    else:
        ckv_ref, kpe_ref, gkva_ref, w_ref, gkn_ref, gkr_ref = refs[:6]
        outs = refs[6:]
    k_ref, v_ref = outs[0], outs[1]
    c = ckv_ref[...]
    if normalize:
        c = c * lax.rsqrt(jnp.mean(c * c, axis=-1, keepdims=True) + EPS) * gkva_ref[...]
        outs[2][...] = c
    kv = jnp.dot(c.astype(BF), w_ref[...], preferred_element_type=F32)
    kpe = kpe_ref[...]
    ss_pe = jnp.sum(kpe * kpe, axis=-1, keepdims=True)
    for h in range(NH_B):
        kn = kv[:, h * (NOPE + VH):h * (NOPE + VH) + NOPE]
        r = lax.rsqrt((jnp.sum(kn * kn, axis=-1, keepdims=True) + ss_pe) * (1.0 / QK_B) + EPS)
        kr = kpe * r * gkr_ref[...]
        if rotate:
            kr = _rotate(kr, cos_ref[...], sa_ref[...], sb_ref[...])
        k_ref[:, h * QK_B_PAD:h * QK_B_PAD + NOPE] = (kn * r * gkn_ref[...]).astype(k_ref.dtype)
        k_ref[:, h * QK_B_PAD + NOPE:(h + 1) * QK_B_PAD] = kr.astype(k_ref.dtype)
        v_ref[:, h * VH:(h + 1) * VH] = kv[:, h * (NOPE + VH) + NOPE:(h + 1) * (NOPE + VH)].astype(v_ref.dtype)


def mla_kv(ckv_src, ckv_blk, kpe_src, kpe_blk, g_kva, w_kvb, g_kn, tables, *, normalize, tm=512):
    n = ckv_src.shape[0]
    rotate = tables is not None
    kw, vw = NH_B * QK_B_PAD, NH_B * VH
    in_specs = [pl.BlockSpec((tm, KV_LORA), lambda i: (i, ckv_blk)), pl.BlockSpec((tm, 128), lambda i: (i, kpe_blk)),
                pl.BlockSpec((1, KV_LORA), lambda i: (0, 0)), pl.BlockSpec((KV_LORA, NH_B * (NOPE + VH)), lambda i: (0, 0)),
                pl.BlockSpec((1, NOPE), lambda i: (0, 0)), pl.BlockSpec((1, 128), lambda i: (0, 0))]
    args = [ckv_src, kpe_src, g_kva.reshape(1, KV_LORA), w_kvb, g_kn[:NOPE].reshape(1, NOPE),
            jnp.pad(g_kn[NOPE:], (0, 128 - ROPE)).reshape(1, 128)]
    if rotate:
        in_specs += [pl.BlockSpec((tm, 128), lambda i: (i, 0))] * 3
        args += list(tables)
    out_shape = [jax.ShapeDtypeStruct((n, kw), BF), jax.ShapeDtypeStruct((n, vw), BF)]
    out_specs = [pl.BlockSpec((tm, kw), lambda i: (i, 0)), pl.BlockSpec((tm, vw), lambda i: (i, 0))]
    if normalize:
        out_shape.append(jax.ShapeDtypeStruct((n, KV_LORA), F32))
        out_specs.append(pl.BlockSpec((tm, KV_LORA), lambda i: (i, 0)))
    return pl.pallas_call(
        functools.partial(_mla_kv_kernel, normalize=normalize, rotate=rotate),
        grid=(n // tm,), in_specs=in_specs, out_specs=out_specs, out_shape=out_shape,
        compiler_params=_cparams(("parallel",)), name="mla_kv",
    )(*args)


def _rotary_tables():
    half, quarter = ROPE // 2, ROPE // 4
    inv = ROPE_BASE ** (-jnp.arange(quarter, dtype=F32) * 2.0 / half)
    t = jnp.arange(LAT_T)
    lane = np.arange(128)
    is_s2 = ((lane % half) >= quarter) & (lane < ROPE)
    is_s1 = ((lane % half) < quarter) & (lane < ROPE)
    pos = jnp.where((lane < half)[None, :], (t // GRID_W)[:, None], (t % GRID_W)[:, None]).astype(F32)
    ang = pos * inv[lane % quarter][None, :]
    live = (lane < ROPE)[None, :]
    cos = jnp.where(live, jnp.cos(ang), 1.0)
    sin = jnp.sin(ang)
    sin_a = jnp.where(is_s1[None, :], -sin, 0.0)
    sin_b = jnp.where(is_s2[None, :], sin, 0.0)

    def full(tab, fill):
        return jnp.concatenate([jnp.full((N_CTX, 128), fill, F32), jnp.tile(tab, (N_LAT_B, 1))], axis=0)

    return full(cos, 1.0), full(sin_a, 0.0), full(sin_b, 0.0)


def _natten_kernel(q_ref, k_ref, v_ref, kc_ref, vc_ref, bias_ref, o_ref, *, scale):
    rb = pl.program_id(2)
    ws = jnp.clip(rb * NA_ROWS - WIN_R // 2, 0, GRID_ROWS - NA_KROWS)
    start = pl.multiple_of(ws * GRID_W, GRID_W)
    nkeys = NA_KROWS * GRID_W
    nt = (((1,), (1,)), ((), ()))
    for h in range(NA_HEADS):
        cs = slice(h * DH_C, (h + 1) * DH_C)
        q = (q_ref[0, :, cs] * (scale * LOG2E)).astype(BF)
        kw = k_ref[0, pl.ds(start, nkeys), cs].astype(BF)
        vw = v_ref[0, pl.ds(start, nkeys), cs].astype(BF)
        s_w = lax.dot_general(q, kw, nt, preferred_element_type=F32) + bias_ref[h, 0]
        s_c = lax.dot_general(q, kc_ref[0, :, cs].astype(BF), nt, preferred_element_type=F32)
        m = jnp.maximum(jnp.max(s_w, axis=1, keepdims=True), jnp.max(s_c, axis=1, keepdims=True))
        p_w = jnp.exp2(s_w - m)
        p_c = jnp.exp2(s_c - m)
        l = jnp.sum(p_w, axis=1, keepdims=True) + jnp.sum(p_c, axis=1, keepdims=True)
        o = jnp.dot(p_w.astype(BF), vw, preferred_element_type=F32)
        o += jnp.dot(p_c.astype(BF), vc_ref[0, :, cs].astype(BF), preferred_element_type=F32)
        o_ref[0, :, cs] = (o / l).astype(o_ref.dtype)


def _natten_bias(rpb):
    n_dr, n_dc = 2 * WIN_R - 1, 2 * WIN_C - 1
    cq = np.arange(GRID_W)[:, None]
    kc = np.arange(GRID_W)[None, :]
    cs = np.clip(cq - WIN_C // 2, 0, GRID_W - WIN_C)
    col_ok = (kc >= cs) & (kc < cs + WIN_C)
    dc = np.clip(kc - cq + WIN_C - 1, 0, n_dc - 1)
    col_sel = (dc[:, :, None] == np.arange(n_dc)).astype(np.float32)
    row_sel, row_ok = [], []
    for r0 in (0, 2 * NA_ROWS, GRID_ROWS - NA_ROWS):
        ws = int(np.clip(r0 - WIN_R // 2, 0, GRID_ROWS - NA_KROWS))
        r = r0 + np.arange(NA_ROWS)[:, None]
        kr = ws + np.arange(NA_KROWS)[None, :]
        rs = np.clip(r - WIN_R // 2, 0, GRID_ROWS - WIN_R)
        row_ok.append((kr >= rs) & (kr < rs + WIN_R))
        dr = np.clip(kr - r + WIN_R - 1, 0, n_dr - 1)
        row_sel.append((dr[:, :, None] == np.arange(n_dr)).astype(np.float32))
    row_sel, row_ok = np.stack(row_sel), np.stack(row_ok)
    hi = lax.Precision.HIGHEST
    t = jnp.einsum("hrc,sijr->hsijc", rpb.astype(F32), row_sel, precision=hi)
    t = jnp.einsum("hsijc,qkc->hsiqjk", t, col_sel, precision=hi)
    ok = row_ok[:, :, None, :, None] & col_ok[None, None, :, None, :]
    t = jnp.where(ok[None], t * LOG2E, NEG)
    return t.reshape(NH_C, 3, NA_ROWS * GRID_W, NA_KROWS * GRID_W)


def natten(qkv, k_ctx, v_ctx, rpb):
    bias = _natten_bias(rpb)
    nrb = GRID_ROWS // NA_ROWS
    tq = NA_ROWS * GRID_W
    nkeys = NA_KROWS * GRID_W

    def case(rb):
        return jnp.where(rb == 0, 0, jnp.where(rb == nrb - 1, 2, 1))

    ng = NH_C // NA_HEADS
    wd = NA_HEADS * DH_C
    return pl.pallas_call(
        functools.partial(_natten_kernel, scale=DH_C ** -0.5),
        grid=(N_LAT_B, ng, nrb),
        in_specs=[
            pl.BlockSpec((1, tq, wd), lambda b, h, rb: (1 + b, rb, h)),
            pl.BlockSpec((1, LAT_T, wd), lambda b, h, rb: (1 + b, 0, ng + h)),
            pl.BlockSpec((1, LAT_T, wd), lambda b, h, rb: (1 + b, 0, 2 * ng + h)),
            pl.BlockSpec((1, PAST, wd), lambda b, h, rb: (b, 0, h)),
            pl.BlockSpec((1, PAST, wd), lambda b, h, rb: (b, 0, h)),
            pl.BlockSpec((NA_HEADS, 1, tq, nkeys), lambda b, h, rb: (h, case(rb), 0, 0)),
        ],
        out_specs=pl.BlockSpec((1, tq, wd), lambda b, h, rb: (b, rb, h)),
        out_shape=jax.ShapeDtypeStruct((N_LAT_B, LAT_T, NH_C * DH_C), BF),
        compiler_params=_cparams(("parallel", "parallel", "arbitrary")), name="natten",
    )(qkv, qkv, qkv, k_ctx, v_ctx, bias)


def _mlstm_kernel(q_ref, k_ref, v_ref, og_ref, gates_ref, c0_ref, nm0_ref, ghn_ref,
                  y_ref, cn_ref, nmn_ref, hf_sc, hb_sc, c_sc, *, nc):
    L = CHUNK
    row = lax.broadcasted_iota(jnp.int32, (L, L), 0)
    col = lax.broadcasted_iota(jnp.int32, (L, L), 1)
    eye = row == col
    nt = (((1,), (1,)), ((), ()))
    tn = (((0,), (0,)), ((), ()))

    def to_col(r):
        return jnp.sum(jnp.where(eye, jnp.broadcast_to(r, (L, L)), 0.0), axis=1, keepdims=True)

    def chunk(c, d, n, m, reverse):
        sl = pl.ds(pl.multiple_of(c * L, L), L)
        g = gates_ref[0, 0, c]
        ic_row = g[2 * d:2 * d + 1, :]
        lf_row = g[2 * d + 1:2 * d + 2, :]
        q = q_ref[0, sl, :]
        k = k_ref[0, sl, :] * (DQK_A ** -0.5)
        v = v_ref[0, sl, :].astype(BF)
        mask = (col >= row) if reverse else (col <= row)
        mask_t = (row >= col) if reverse else (row <= col)
        lf_b = jnp.broadcast_to(lf_row, (L, L))
        b_col = jnp.sum(jnp.where(mask, lf_b, 0.0), axis=1, keepdims=True)
        lf_col = jnp.sum(jnp.where(eye, lf_b, 0.0), axis=1, keepdims=True)
        b_row = jnp.sum(jnp.where(mask_t, lf_col, 0.0), axis=0, keepdims=True)
        ic_col = to_col(ic_row)
        dmat = jnp.where(mask, b_col - b_row + ic_row, NEG)
        inter = b_col + m
        m_t = jnp.maximum(inter, jnp.max(dmat, axis=1, keepdims=True))
        a = jnp.exp(inter - m_t)
        qb = q.astype(BF)
        s = lax.dot_general(qb, k.astype(BF), nt, preferred_element_type=F32) * jnp.exp(dmat - m_t)
        c_old = c_sc[d]
        num = a * jnp.dot(qb, c_old.astype(BF), preferred_element_type=F32)
        num += jnp.dot(s.astype(BF), v, preferred_element_type=F32)
        den = a * jnp.sum(q * n, axis=1, keepdims=True) + jnp.sum(s, axis=1, keepdims=True)
        h = num / jnp.maximum(jnp.abs(den), jnp.exp(-m_t))
        b_end = jnp.sum(lf_row, axis=1, keepdims=True)
        g_col = b_end - b_col + ic_col
        m_new = jnp.maximum(b_end + m, jnp.max(g_col, axis=0, keepdims=True))
        a_s = jnp.exp(b_end + m - m_new)
        kw = k * jnp.exp(g_col - m_new)
        c_sc[d] = a_s * c_old + lax.dot_general(kw.astype(BF), v, tn, preferred_element_type=F32)
        n_new = a_s * n + jnp.sum(kw, axis=0, keepdims=True)
        return h, sl, n_new, m_new

    c_sc[0] = c0_ref[0, 0, 0]
    c_sc[1] = c0_ref[0, 1, 0]
    init = (nm0_ref[0, 0, 0, 0:1, :], nm0_ref[0, 0, 0, 1:2, 0:1],
            nm0_ref[0, 1, 0, 0:1, :], nm0_ref[0, 1, 0, 1:2, 0:1])

    def body(i, carry):
        n_f, m_f, n_b, m_b = carry
        h, sl, n_f, m_f = chunk(i, 0, n_f, m_f, False)
        hf_sc[sl, :] = h
        h, sl, n_b, m_b = chunk(nc - 1 - i, 1, n_b, m_b, True)
        hb_sc[sl, :] = h
        return n_f, m_f, n_b, m_b

    n_f, m_f, n_b, m_b = lax.fori_loop(0, nc, body, init, unroll=2)

    cn_ref[0, 0, 0] = c_sc[0]
    cn_ref[0, 1, 0] = c_sc[1]
    for d, (n_d, m_d) in enumerate(((n_f, m_f), (n_b, m_b))):
        nmn_ref[0, d, 0] = jnp.concatenate(
            [n_d, jnp.broadcast_to(m_d, (1, DQK_A)), jnp.zeros((6, DQK_A), F32)], axis=0)

    rows = 256
    gain = ghn_ref[0]

    def out_body(i, _):
        sl = pl.ds(pl.multiple_of(i * rows, rows), rows)
        hs = hf_sc[sl, :] + hb_sc[sl, :]
        hn = hs * lax.rsqrt(jnp.mean(hs * hs, axis=-1, keepdims=True) + EPS) * gain
        gate = 1.0 / (1.0 + jnp.exp(-og_ref[0, sl, :]))
        y_ref[0, sl, :] = (gate * hn).astype(y_ref.dtype)
        return 0

    lax.fori_loop(0, (nc * L) // rows, out_body, 0)


def mlstm(proj, gates, c0, nm0, g_hn, *, nb, t_len, boff):
    nc = t_len // CHUNK
    kq = A_QK // DQK_A
    kv = 2 * A_QK // DV_A
    ko = (2 * A_QK + A_V) // DV_A
    return pl.pallas_call(
        functools.partial(_mlstm_kernel, nc=nc),
        grid=(nb, NH_A),
        in_specs=[
            pl.BlockSpec((1, t_len, DQK_A), lambda b, h: (boff + b, 0, h)),
            pl.BlockSpec((1, t_len, DQK_A), lambda b, h: (boff + b, 0, kq + h)),
            pl.BlockSpec((1, t_len, DV_A), lambda b, h: (boff + b, 0, kv + h)),
            pl.BlockSpec((1, t_len, DV_A), lambda b, h: (boff + b, 0, ko + h)),
            pl.BlockSpec((1, 1, nc, 4, CHUNK), lambda b, h: (b, h, 0, 0, 0)),
            pl.BlockSpec((1, 2, 1, DQK_A, DV_A), lambda b, h: (b, 0, h, 0, 0)),
            pl.BlockSpec((1, 2, 1, 8, DQK_A), lambda b, h: (b, 0, h, 0, 0)),
            pl.BlockSpec((1, 1, DV_A), lambda b, h: (h, 0, 0)),
        ],
        out_specs=[
            pl.BlockSpec((1, t_len, DV_A), lambda b, h: (b, 0, h)),
            pl.BlockSpec((1, 2, 1, DQK_A, DV_A), lambda b, h: (b, 0, h, 0, 0)),
            pl.BlockSpec((1, 2, 1, 8, DQK_A), lambda b, h: (b, 0, h, 0, 0)),
        ],
        out_shape=[
            jax.ShapeDtypeStruct((nb, t_len, A_V), BF),
            jax.ShapeDtypeStruct((nb, 2, NH_A, DQK_A, DV_A), F32),
            jax.ShapeDtypeStruct((nb, 2, NH_A, 8, DQK_A), F32),
        ],
        scratch_shapes=[pltpu.VMEM((t_len, DV_A), F32), pltpu.VMEM((t_len, DV_A), F32),
                        pltpu.VMEM((2, DQK_A, DV_A), F32)],
        compiler_params=_cparams(("parallel", "parallel")), name="mlstm",
    )(proj, proj, proj, proj, gates, c0, nm0, g_hn.reshape(NH_A, 1, DV_A))


def _w1_prep_kernel(w_ref, p_ref, o_ref):
    gw = 2 * MOE_GRP
    for c in range(w_ref.shape[2] // gw):
        blk = w_ref[0, :, c * gw:(c + 1) * gw].astype(BF)
        o_ref[0, :, c * gw:(c + 1) * gw] = jnp.dot(blk, p_ref[...], preferred_element_type=F32).astype(BF)


def _deinterleave_perm():
    gw = 2 * MOE_GRP
    j = np.arange(gw)
    src = np.where(j < MOE_GRP, 2 * j, 2 * (j - MOE_GRP) + 1)
    perm = np.zeros((gw, gw), np.float32)
    perm[src, j] = 1.0
    return jnp.asarray(perm, BF)


def w1_prep(w1):
    ne, d, n2 = w1.shape
    tn = 1024
    gw = 2 * MOE_GRP
    return pl.pallas_call(
        _w1_prep_kernel,
        grid=(ne, n2 // tn),
        in_specs=[pl.BlockSpec((1, d, tn), lambda e, j: (e, 0, j)), pl.BlockSpec((gw, gw), lambda e, j: (0, 0))],
        out_specs=pl.BlockSpec((1, d, tn), lambda e, j: (e, 0, j)),
        out_shape=jax.ShapeDtypeStruct((ne, d, n2), BF),
        compiler_params=_cparams(("parallel", "parallel")), name="w1_prep",
    )(w1, _deinterleave_perm())


def _cast_kernel(x_ref, o_ref):
    o_ref[...] = x_ref[...].astype(o_ref.dtype)


def cast_bf16(x, rows):
    ne, r, c = x.shape
    return pl.pallas_call(
        _cast_kernel,
        grid=(ne, r // rows),
        in_specs=[pl.BlockSpec((1, rows, c), lambda e, j: (e, j, 0))],
        out_specs=pl.BlockSpec((1, rows, c), lambda e, j: (e, j, 0)),
        out_shape=jax.ShapeDtypeStruct(x.shape, BF),
        compiler_params=_cparams(("parallel", "parallel")), name="cast_bf16",
    )(x)


def _moe_kernel(blk_e_ref, nused_ref, x_ref, w1_ref, b1_ref, w2_ref, b2_ref, o_ref):
    b = pl.program_id(0)
    f = pl.program_id(1)

    @pl.when(f == 0)
    def _():
        o_ref[...] = jnp.broadcast_to(b2_ref[0], o_ref.shape)

    @pl.when(b < nused_ref[0])
    def _():
        x = _unpack_bf16_pairs(x_ref[...])
        gw = 2 * MOE_GRP
        part = None
        for c in range(MOE_TF // MOE_GRP):
            hh = jnp.dot(x, w1_ref[0, :, c * gw:(c + 1) * gw], preferred_element_type=F32)
            hh = hh + b1_ref[0, :, c * gw:(c + 1) * gw]
            glu = jnp.minimum(hh[:, :MOE_GRP], SWIGLU_LIMIT)
            lin = jnp.clip(hh[:, MOE_GRP:], -SWIGLU_LIMIT, SWIGLU_LIMIT)
            act = glu * (1.0 / (1.0 + jnp.exp(-SWIGLU_ALPHA * glu))) * (lin + 1.0)
            p = jnp.dot(act.astype(BF), w2_ref[0, c * MOE_GRP:(c + 1) * MOE_GRP, :], preferred_element_type=F32)
            part = p if part is None else part + p
        o_ref[...] += part


def moe_experts(xb, blk_e, nused, w1p, b1p, w2, b2):
    nf = D_FF // MOE_TF
    d = D_MODEL

    def eidx(b, be):
        return be[b]

    def fidx(b, f, nu):
        return jnp.where(b < nu[0], f, nf - 1)

    grid_spec = pltpu.PrefetchScalarGridSpec(
        num_scalar_prefetch=2,
        grid=(MOE_NB, nf),
        in_specs=[
            pl.BlockSpec((MOE_BM, d // 2), lambda b, f, be, nu: (jnp.minimum(b, nu[0] - 1), 0)),
            pl.BlockSpec((1, d, 2 * MOE_TF), lambda b, f, be, nu: (eidx(b, be), 0, fidx(b, f, nu))),
            pl.BlockSpec((1, 1, 2 * MOE_TF), lambda b, f, be, nu: (eidx(b, be), 0, fidx(b, f, nu))),
            pl.BlockSpec((1, MOE_TF, d), lambda b, f, be, nu: (eidx(b, be), fidx(b, f, nu), 0)),
            pl.BlockSpec((1, 1, d), lambda b, f, be, nu: (eidx(b, be), 0, 0)),
        ],
        out_specs=pl.BlockSpec((MOE_BM, d), lambda b, f, be, nu: (b, 0)),
    )
    return pl.pallas_call(
        _moe_kernel,
        grid_spec=grid_spec,
        out_shape=jax.ShapeDtypeStruct((MOE_NB * MOE_BM, d), F32),
        compiler_params=_cparams(("arbitrary", "arbitrary")), name="moe_experts",
    )(blk_e, nused, xb, w1p, b1p, w2, b2)


def moe(hdn, logits, w1, b1, w2, b2):
    n = hdn.shape[0]
    nk = n * TOP_K
    top_v, top_i = lax.top_k(logits, TOP_K)
    gate = jax.nn.softmax(top_v, axis=-1)
    e = top_i.reshape(-1).astype(jnp.int32)
    rb = 512
    onehot = (e[None, :] == jnp.arange(N_EXP, dtype=jnp.int32)[:, None]).astype(BF).reshape(N_EXP, nk // rb, rb)
    within = jnp.einsum("ebs,ts->ebt", onehot, jnp.tril(jnp.ones((rb, rb), BF)), preferred_element_type=F32)
    tot = within[:, :, -1]
    before = jnp.cumsum(tot, axis=1) - tot
    rank = jnp.sum((within + before[:, :, None]) * onehot.astype(F32), axis=0).reshape(nk).astype(jnp.int32) - 1
    counts = (before[:, -1] + tot[:, -1]).astype(jnp.int32)
    padded = (counts + MOE_BM - 1) // MOE_BM * MOE_BM
    pad_end = jnp.cumsum(padded)
    pad_start = pad_end - padded
    dest = pad_start[e] + rank
    tok = jnp.arange(nk, dtype=jnp.int32) // TOP_K
    slot_tok = (jnp.arange(MOE_NB * MOE_BM, dtype=jnp.int32) % n).at[dest].set(
        tok, unique_indices=True, mode="promise_in_bounds")
    blk_start = jnp.arange(MOE_NB, dtype=jnp.int32) * MOE_BM
    blk_e = jnp.minimum(jnp.sum((pad_end[None, :] <= blk_start[:, None]).astype(jnp.int32), axis=1), N_EXP - 1)
    nused = (pad_end[-1] // MOE_BM).astype(jnp.int32).reshape(1)
    xb = hdn.at[slot_tok].get(mode="promise_in_bounds")
    b1p = b1.reshape(N_EXP, D_FF // MOE_GRP, MOE_GRP, 2).transpose(0, 1, 3, 2).reshape(N_EXP, 1, 2 * D_FF)
    yb = moe_experts(xb, blk_e, nused, w1_prep(w1), b1p, cast_bf16(w2, MOE_TF), b2[:, None, :])
    return yb, dest.reshape(n, TOP_K), gate


def moe_combine(yb, dest, gate, lo, hi):
    y = None
    for k in range(TOP_K):
        yk = yb.at[dest[lo:hi, k]].get(mode="promise_in_bounds") * gate[lo:hi, k:k + 1]
        y = yk if y is None else y + yk
    return y


def _pad_cols(w, width):
    return jnp.pad(w, ((0, 0), (0, width - w.shape[1])))


def _adaln(cond, ada_w, ada_b):
    a = jnp.pad(jax.nn.silu(cond), ((0, 8 - cond.shape[0]), (0, 0)))
    mod = matmul(a, ada_w, bias=ada_b, tm=8, tn=1024)
    return mod[:cond.shape[0]].reshape(cond.shape[0], 6, D_MODEL)


def _layer0_mixer(x, mod, g_mix, w_in, b_gate, g_qa, w_qb, g_kva, w_kvb, g_qn, g_kn, g_hn, w_out, cache):
    st_c, st_n, st_m, c_ckv, c_kpe = cache
    o_gate = 2 * A_QK + 2 * A_V
    o_cq = o_gate + N_GATE
    w_gate = _pad_cols(w_in[:, o_gate:o_cq], AUX_W)
    hdn, gates = norm_mod(x, g_mix, mod, 0, aux_w=w_gate, aux_b=jnp.pad(b_gate, (0, AUX_W - N_GATE)))
    proj = matmul(hdn, w_in[:, :o_gate].astype(BF))
    w_b = _pad_cols(w_in[:, o_cq:], MLA_IN_W).astype(BF)
    proj_b = matmul(hdn, w_b, tn=MLA_IN_W)

    gt = GATE_CAP * jnp.tanh(gates[:, :N_GATE] / GATE_CAP)
    gt = gt.reshape(N_TOK, 4, NH_A)
    gt = jnp.stack([gt[:, 0], jax.nn.log_sigmoid(gt[:, 1]), gt[:, 2], jax.nn.log_sigmoid(gt[:, 3])], axis=1)

    def gate_layout(g, nb, t_len):
        g = g.reshape(nb, t_len // CHUNK, CHUNK, 4, NH_A)
        return g.transpose(0, 4, 1, 3, 2)

    zc = jnp.zeros((N_CTX_B, 2, NH_A, DQK_A, DV_A), F32)
    znm = jnp.zeros((N_CTX_B, 2, NH_A, 8, DQK_A), F32)
    y_ctx, new_c, new_nm = mlstm(proj.reshape(N_TOK // CTX_T, CTX_T, -1), gate_layout(gt[:N_CTX], N_CTX_B, CTX_T),
                                 zc, znm, g_hn, nb=N_CTX_B, t_len=CTX_T, boff=0)
    nm_lat = jnp.concatenate([st_n[:, :, :, None, :],
                              jnp.broadcast_to(st_m[:, :, :, None, None], (N_LAT_B, 2, NH_A, 1, DQK_A)),
                              jnp.zeros((N_LAT_B, 2, NH_A, 6, DQK_A), F32)], axis=3)
    y_lat, _, _ = mlstm(proj.reshape(N_TOK // LAT_T, LAT_T, -1), gate_layout(gt[N_CTX:], N_LAT_B, LAT_T),
                        st_c, nm_lat, g_hn, nb=N_LAT_B, t_len=LAT_T, boff=1)
    y_a = jnp.concatenate([y_ctx.reshape(N_CTX, A_V), y_lat.reshape(N_LAT, A_V)], axis=0)
    new_n = new_nm[:, :, :, 0, :]
    new_m = new_nm[:, :, :, 1, 0]

    wq = w_qb.reshape(Q_LORA, NH_B, QK_B)
    wq = jnp.pad(wq, ((0, 0), (0, 0), (0, QK_B_PAD - QK_B))).reshape(Q_LORA, NH_B * QK_B_PAD).astype(BF)
    gq = jnp.tile(jnp.pad(g_qn, (0, QK_B_PAD - QK_B)), NH_B)
    tables = _rotary_tables()
    w_kvb_b = w_kvb.astype(BF)
    q = mla_q(proj_b, g_qa, wq, gq, tables)
    k_new, v_new, ckv_n = mla_kv(proj_b, 1, proj_b, (Q_LORA + KV_LORA) // 128, g_kva, w_kvb_b, g_kn, tables,
                                 normalize=True)
    k_old, v_old = mla_kv(c_ckv.reshape(N_LAT_B * PAST, KV_LORA), 0,
                          _pad_cols(c_kpe.reshape(N_LAT_B * PAST, ROPE), 128), 0, g_kva, w_kvb_b, g_kn, None,
                          normalize=False)
    kpe = proj_b[:N_CTX, Q_LORA + KV_LORA:Q_LORA + KV_LORA + ROPE]
    ident = lambda b, h: (b, h)
    lat = lambda b, h: (1 + b, h)
    kw, vw = NH_B * QK_B_PAD, NH_B * VH
    y_b_ctx = attention(
        q.reshape(N_TOK // CTX_T, CTX_T, kw), ident,
        [(k_new.reshape(N_TOK // CTX_T, CTX_T, kw), ident, v_new.reshape(N_TOK // CTX_T, CTX_T, vw), ident, CTX_T)],
        nb=N_CTX_B, nh=NH_B, tq_len=CTX_T, dk=QK_B_PAD, dv=VH, tq=CTX_T, q_scale=None)
    y_b_lat = attention(
        q.reshape(N_TOK // LAT_T, LAT_T, kw), lat,
        [(k_new.reshape(N_TOK // LAT_T, LAT_T, kw), lat, v_new.reshape(N_TOK // LAT_T, LAT_T, vw), lat, LAT_T),
         (k_old.reshape(N_LAT_B, PAST, kw), ident, v_old.reshape(N_LAT_B, PAST, vw), ident, PAST)],
        nb=N_LAT_B, nh=NH_B, tq_len=LAT_T, dk=QK_B_PAD, dv=VH, tq=256, q_scale=None)
    y_b = jnp.concatenate([y_b_ctx.reshape(N_CTX, -1), y_b_lat.reshape(N_LAT, -1)], axis=0)

    y = jnp.concatenate([y_a, y_b], axis=1)
    x = matmul(y, w_out.astype(BF), resid=x, mod=mod, gate_row=2)
    new = (new_c, new_n, new_m, ckv_n[:N_CTX].reshape(N_CTX_B, CTX_T, KV_LORA), kpe.reshape(N_CTX_B, CTX_T, ROPE))
    return x, new


def _layer1_mixer(x, mod, g_mix, w_qkv, g_qn, g_kn, rpb, w_out, cache):
    c_k, c_v = cache
    hd = NH_C * DH_C
    hdn = norm_mod(x, g_mix, mod, 0)
    gain = jnp.concatenate([jnp.tile(g_qn, NH_C), jnp.tile(g_kn, NH_C), jnp.ones((hd,), F32)])
    qkv = matmul(hdn, w_qkv.astype(BF), gain=gain, norm_group=DH_C, norm_div=float(DH_C), n_norm_cols=2 * hd)
    new_k = qkv[:N_CTX, hd:2 * hd].reshape(N_CTX_B, CTX_T, NH_C, DH_C)
    new_v = qkv[:N_CTX, 2 * hd:].reshape(N_CTX_B, CTX_T, NH_C, DH_C)
    qkv_c = qkv.reshape(N_TOK // CTX_T, CTX_T, 3 * hd)
    o_ctx = attention(
        qkv_c, lambda b, h: (b, h),
        [(qkv_c, lambda b, h: (b, NH_C + h), qkv_c, lambda b, h: (b, 2 * NH_C + h), CTX_T)],
        nb=N_CTX_B, nh=NH_C, tq_len=CTX_T, dk=DH_C, dv=DH_C, tq=CTX_T, q_scale=DH_C ** -0.5)
    o_lat = natten(qkv.reshape(N_TOK // LAT_T, LAT_T, 3 * hd), c_k.reshape(N_LAT_B, PAST, hd),
                   c_v.reshape(N_LAT_B, PAST, hd), rpb)
    o = jnp.concatenate([o_ctx.reshape(N_CTX, hd), o_lat.reshape(N_LAT, hd)], axis=0)
    x = matmul(o, w_out.astype(BF), resid=x, mod=mod, gate_row=2)
    return x, (new_k, new_v)


def _moe_sublayer(x, mod, g_moe, w_r, b_r, w1, b1, w2, b2, split=False):
    hdn, logits = norm_mod(x, g_moe, mod, 1, aux_w=_pad_cols(w_r, AUX_W), aux_b=jnp.pad(b_r, (0, AUX_W - N_EXP)),
                           packed=True)
    yb, dest, gate = moe(hdn, logits[:, :N_EXP], w1, b1, w2, b2)
    ga = mod[:, 5, :]
    if not split:
        ga_rows = jnp.repeat(ga, N_CTX, axis=0, total_repeat_length=N_TOK)
        return x + ga_rows * moe_combine(yb, dest, gate, 0, N_TOK)
    y_ctx = x[:N_CTX] + ga[0][None, :] * moe_combine(yb, dest, gate, 0, N_CTX)
    y_lat = x[N_CTX:].reshape(N_LAT_B, LAT_T, D_MODEL) + ga[1:, None, :] * moe_combine(
        yb, dest, gate, N_CTX, N_TOK).reshape(N_LAT_B, LAT_T, D_MODEL)
    return y_ctx, y_lat.reshape(N_LAT, D_MODEL)


def kernel(x_prompt, x_sample, state_l0_mlstm_C, state_l0_mlstm_n, state_l0_mlstm_m, cache_l0_mla_ckv, cache_l0_mla_kpe, cache_l1_na_k, cache_l1_na_v, c, c_ctx, l0_g_mix, l0_g_moe, l0_ada_w, l0_ada_b, l0_w_in, l0_b_gate, l0_g_qa, l0_w_qb, l0_g_kva, l0_w_kvb, l0_g_qn, l0_g_kn, l0_g_hn, l0_w_out, l0_w_router, l0_b_router, l0_w1, l0_b1, l0_w2, l0_b2, l1_g_mix, l1_g_moe, l1_ada_w, l1_ada_b, l1_w_qkv, l1_g_qn, l1_g_kn, l1_rpb, l1_w_out, l1_w_router, l1_b_router, l1_w1, l1_b1, l1_w2, l1_b2):
    x = jnp.concatenate([x_prompt.reshape(N_CTX, D_MODEL), x_sample.reshape(N_LAT, D_MODEL)], axis=0)
    cond = jnp.concatenate([c_ctx[None, :], c], axis=0)

    mod0 = _adaln(cond, l0_ada_w, l0_ada_b)
    x, (new_c, new_n, new_m, new_ckv, new_kpe) = _layer0_mixer(
        x, mod0, l0_g_mix, l0_w_in, l0_b_gate, l0_g_qa, l0_w_qb, l0_g_kva, l0_w_kvb, l0_g_qn, l0_g_kn, l0_g_hn,
        l0_w_out, (state_l0_mlstm_C, state_l0_mlstm_n, state_l0_mlstm_m, cache_l0_mla_ckv, cache_l0_mla_kpe))
    x = _moe_sublayer(x, mod0, l0_g_moe, l0_w_router, l0_b_router, l0_w1, l0_b1, l0_w2, l0_b2)

    mod1 = _adaln(cond, l1_ada_w, l1_ada_b)
    x, (new_k, new_v) = _layer1_mixer(x, mod1, l1_g_mix, l1_w_qkv, l1_g_qn, l1_g_kn, l1_rpb, l1_w_out,
                                      (cache_l1_na_k, cache_l1_na_v))
    y_ctx, y_lat = _moe_sublayer(x, mod1, l1_g_moe, l1_w_router, l1_b_router, l1_w1, l1_b1, l1_w2, l1_b2, split=True)

    y_prompt = y_ctx.reshape(N_CTX_B, CTX_T, D_MODEL)
    y_sample = y_lat.reshape(N_LAT_B, LAT_T, D_MODEL)
    return (y_prompt, y_sample, new_c, new_n, new_m, new_ckv, new_kpe, new_k, new_v)
```

```python
import functools

import numpy as np
import jax
import jax.numpy as jnp
from jax import lax
from jax.experimental import pallas as pl
from jax.experimental.pallas import tpu as pltpu

BF = jnp.bfloat16
F32 = jnp.float32

D_MODEL = 2048
N_CTX_B, CTX_T = 16, 256
N_LAT_B, LAT_T = 4, 4096
N_CTX = N_CTX_B * CTX_T
N_LAT = N_LAT_B * LAT_T
N_TOK = N_CTX + N_LAT
PAST = 512
GRID_W = 64
GRID_ROWS = LAT_T // GRID_W
EPS = 1e-6
NEG = -1e30

NH_A, DQK_A, DV_A = 4, 128, 256
A_QK, A_V = NH_A * DQK_A, NH_A * DV_A
N_GATE = 4 * NH_A
CHUNK = 64
GATE_CAP = 15.0
NH_B, Q_LORA, KV_LORA, NOPE, ROPE, VH = 8, 512, 512, 128, 64, 128
QK_B = NOPE + ROPE
QK_B_PAD = 256
ROPE_BASE = 10000.0
NH_C, DH_C = 16, 128
WIN_R, WIN_C = 8, 16
NA_ROWS = 4
NA_KROWS = 12
NA_HEADS = 4
LOG2E = 1.4426950408889634
N_EXP, TOP_K, D_FF = 32, 4, 2048
SWIGLU_ALPHA, SWIGLU_LIMIT = 1.702, 7.0
MOE_BM = 512
MOE_TF = 2048
MOE_GRP = 256
MOE_WBUF = 1
MOE_NB = (N_TOK * TOP_K + N_EXP * (MOE_BM - 1) + MOE_BM - 1) // MOE_BM
AUX_W = 128
MLA_IN_W = 1280

VMEM_LIMIT = 56 * 1024 * 1024


def _cparams(sem):
    return pltpu.CompilerParams(dimension_semantics=sem, vmem_limit_bytes=VMEM_LIMIT)


def _split_bf16(x):
    hi = x.astype(BF)
    lo = (x - hi.astype(F32)).astype(BF)
    return hi, lo


def _pack_bf16_pairs(h):
    c = h.shape[1] // 2
    hb = h.astype(BF).astype(F32)
    hi = lax.bitcast_convert_type(hb[:, :c], jnp.uint32)
    lo = lax.bitcast_convert_type(hb[:, c:], jnp.uint32)
    return hi | (lo >> 16)


def _unpack_bf16_pairs(w):
    hi = lax.bitcast_convert_type(w & jnp.uint32(0xFFFF0000), F32).astype(BF)
    lo = lax.bitcast_convert_type(w << 16, F32).astype(BF)
    return jnp.concatenate([hi, lo], axis=1)


def _norm_mod_kernel(*refs, which, has_aux, packed):
    if has_aux:
        x_ref, g_ref, mod_ref, w_ref, b_ref, o_ref, aux_ref = refs
    else:
        x_ref, g_ref, mod_ref, o_ref = refs
    x = x_ref[...]
    ms = jnp.mean(x * x, axis=-1, keepdims=True)
    y = x * lax.rsqrt(ms + EPS) * g_ref[...]
    sh = mod_ref[0, 3 * which:3 * which + 1, :]
    sc = mod_ref[0, 3 * which + 1:3 * which + 2, :]
    h = y * (1.0 + sc) + sh
    o_ref[...] = _pack_bf16_pairs(h) if packed else h.astype(BF)
    if has_aux:
        h_hi, h_lo = _split_bf16(h)
        w_hi, w_lo = _split_bf16(w_ref[...])
        acc = jnp.dot(h_hi, w_hi, preferred_element_type=F32)
        acc += jnp.dot(h_hi, w_lo, preferred_element_type=F32)
        acc += jnp.dot(h_lo, w_hi, preferred_element_type=F32)
        aux_ref[...] = acc + b_ref[...]


def norm_mod(x, g, mod, which, aux_w=None, aux_b=None, packed=False, tm=512):
    n, d = x.shape
    has_aux = aux_w is not None
    in_specs = [
        pl.BlockSpec((tm, d), lambda i: (i, 0)),
        pl.BlockSpec((1, d), lambda i: (0, 0)),
        pl.BlockSpec((1, 6, d), lambda i: ((i * tm) // N_CTX, 0, 0)),
    ]
    args = [x, g.reshape(1, d), mod]
    od, odt = (d // 2, jnp.uint32) if packed else (d, BF)
    out_shape = [jax.ShapeDtypeStruct((n, od), odt)]
    out_specs = [pl.BlockSpec((tm, od), lambda i: (i, 0))]
    if has_aux:
        in_specs += [pl.BlockSpec((d, AUX_W), lambda i: (0, 0)), pl.BlockSpec((1, AUX_W), lambda i: (0, 0))]
        args += [aux_w, aux_b.reshape(1, AUX_W)]
        out_shape.append(jax.ShapeDtypeStruct((n, AUX_W), F32))
        out_specs.append(pl.BlockSpec((tm, AUX_W), lambda i: (i, 0)))
    res = pl.pallas_call(
        functools.partial(_norm_mod_kernel, which=which, has_aux=has_aux, packed=packed),
        grid=(n // tm,), in_specs=in_specs, out_specs=out_specs, out_shape=out_shape,
        compiler_params=_cparams(("parallel",)), name="norm_mod",
    )(*args)
    return res if has_aux else res[0]


def _group_rmsnorm(acc, gain, group, div):
    parts = []
    for c in range(acc.shape[1] // group):
        blk = acc[:, c * group:(c + 1) * group]
        ms = jnp.sum(blk * blk, axis=-1, keepdims=True) * (1.0 / div)
        parts.append(blk * lax.rsqrt(ms + EPS) * gain[:, c * group:(c + 1) * group])
    return parts[0] if len(parts) == 1 else jnp.concatenate(parts, axis=-1)


def _mm_kernel(*refs, has_bias, gate_row, norm_group, norm_div, n_norm_tiles):
    a_ref, w_ref = refs[0], refs[1]
    pos = 2
    if has_bias:
        b_ref = refs[pos]; pos += 1
    if gate_row is not None:
        x_ref, mod_ref = refs[pos], refs[pos + 1]; pos += 2
    if norm_group:
        gain_ref = refs[pos]; pos += 1
    o_ref = refs[pos]

    acc = jnp.dot(a_ref[...].astype(BF), w_ref[...].astype(BF), preferred_element_type=F32)
    if has_bias:
        acc = acc + b_ref[...]
    if gate_row is not None:
        acc = x_ref[...] + mod_ref[0, gate_row:gate_row + 1, :] * acc
    if norm_group:
        j = pl.program_id(1)

        @pl.when(j < n_norm_tiles)
        def _():
            o_ref[...] = _group_rmsnorm(acc, gain_ref[...], norm_group, norm_div).astype(o_ref.dtype)

        @pl.when(j >= n_norm_tiles)
        def _():
            o_ref[...] = acc.astype(o_ref.dtype)
    else:
        o_ref[...] = acc.astype(o_ref.dtype)


def matmul(a, w, *, bias=None, resid=None, mod=None, gate_row=None, gain=None, norm_group=0,
           norm_div=1.0, n_norm_cols=None, out_dtype=F32, tm=1024, tn=512):
    m, k = a.shape
    n = w.shape[1]
    tm = min(tm, m)
    tn = min(tn, n)
    assert m % tm == 0 and n % tn == 0
    in_specs = [pl.BlockSpec((tm, k), lambda i, j: (i, 0)), pl.BlockSpec((k, tn), lambda i, j: (0, j))]
    args = [a, w]
    if bias is not None:
        in_specs.append(pl.BlockSpec((1, tn), lambda i, j: (0, j)))
        args.append(bias.reshape(1, n))
    if gate_row is not None:
        in_specs.append(pl.BlockSpec((tm, tn), lambda i, j: (i, j)))
        in_specs.append(pl.BlockSpec((1, 6, tn), lambda i, j: ((i * tm) // N_CTX, 0, j)))
        args += [resid, mod]
    n_norm_tiles = 0
    if norm_group:
        n_norm_cols = n if n_norm_cols is None else n_norm_cols
        assert n_norm_cols % tn == 0 and tn % norm_group == 0
        n_norm_tiles = n_norm_cols // tn
        in_specs.append(pl.BlockSpec((1, tn), lambda i, j: (0, j)))
        args.append(gain.reshape(1, n))
    return pl.pallas_call(
        functools.partial(_mm_kernel, has_bias=bias is not None, gate_row=gate_row, norm_group=norm_group,
                          norm_div=norm_div, n_norm_tiles=n_norm_tiles),
        grid=(m // tm, n // tn), in_specs=in_specs,
        out_specs=pl.BlockSpec((tm, tn), lambda i, j: (i, j)),
        out_shape=jax.ShapeDtypeStruct((m, n), out_dtype),
        compiler_params=_cparams(("parallel", "parallel")), name="matmul",
    )(*args)


ATTN_CHUNK = 512


def _attn_kernel(*refs, q_scale, n_src):
    q_ref, o_ref = refs[0], refs[-1]
    q = q_ref[0]
    if q_scale is not None:
        q = q * (q_scale * LOG2E)
    q = q.astype(BF)
    nt = (((1,), (1,)), ((), ()))
    scores, values = [], []
    for i in range(n_src):
        k_ref, v_ref = refs[1 + 2 * i], refs[2 + 2 * i]
        tk = k_ref.shape[1]
        ch = min(ATTN_CHUNK, tk)
        for c in range(tk // ch):
            k = k_ref[0, c * ch:(c + 1) * ch, :].astype(BF)
            scores.append(lax.dot_general(q, k, nt, preferred_element_type=F32))
            values.append((v_ref, c * ch, ch))
    m = functools.reduce(jnp.maximum, [jnp.max(s, axis=1, keepdims=True) for s in scores])
    l, acc = None, None
    for s, (v_ref, start, ch) in zip(scores, values):
        p = jnp.exp2(s - m)
        ps = jnp.sum(p, axis=1, keepdims=True)
        pv = jnp.dot(p.astype(BF), v_ref[0, start:start + ch, :].astype(BF), preferred_element_type=F32)
        l = ps if l is None else l + ps
        acc = pv if acc is None else acc + pv
    o_ref[0] = (acc / l).astype(o_ref.dtype)


def attention(q, q_map, kv_sources, *, nb, nh, tq_len, dk, dv, tq, q_scale):
    in_specs = [pl.BlockSpec((1, tq, dk), lambda b, h, qi: (q_map(b, h)[0], qi, q_map(b, h)[1]))]
    args = [q]
    for k, k_map, v, v_map, n_keys in kv_sources:
        in_specs.append(pl.BlockSpec((1, n_keys, dk), lambda b, h, qi, f=k_map: (f(b, h)[0], 0, f(b, h)[1])))
        in_specs.append(pl.BlockSpec((1, n_keys, dv), lambda b, h, qi, f=v_map: (f(b, h)[0], 0, f(b, h)[1])))
        args += [k, v]
    return pl.pallas_call(
        functools.partial(_attn_kernel, q_scale=q_scale, n_src=len(kv_sources)),
        grid=(nb, nh, tq_len // tq), in_specs=in_specs,
        out_specs=pl.BlockSpec((1, tq, dv), lambda b, h, qi: (b, qi, h)),
        out_shape=jax.ShapeDtypeStruct((nb, tq_len, nh * dv), BF),
        compiler_params=_cparams(("parallel", "parallel", "arbitrary")), name="attention",
    )(*args)


def _rotate(x, cos, sin_a, sin_b):
    quarter = ROPE // 4
    return x * cos + pltpu.roll(x, 128 - quarter, 1) * sin_a + pltpu.roll(x, quarter, 1) * sin_b


def _mla_q_kernel(cq_ref, gqa_ref, wq_ref, gq_ref, cos_ref, sa_ref, sb_ref, o_ref, *, out_scale):
    cq = cq_ref[...]
    cqn = cq * lax.rsqrt(jnp.mean(cq * cq, axis=-1, keepdims=True) + EPS) * gqa_ref[...]
    q = jnp.dot(cqn.astype(BF), wq_ref[...], preferred_element_type=F32)
    cos, sa, sb = cos_ref[...], sa_ref[...], sb_ref[...]
    for h in range(NH_B):
        blk = q[:, h * QK_B_PAD:(h + 1) * QK_B_PAD]
        r = lax.rsqrt(jnp.sum(blk * blk, axis=-1, keepdims=True) * (1.0 / QK_B) + EPS)
        y = blk * r * gq_ref[:, h * QK_B_PAD:(h + 1) * QK_B_PAD]
        o_ref[:, h * QK_B_PAD:h * QK_B_PAD + NOPE] = (y[:, :NOPE] * out_scale).astype(o_ref.dtype)
        o_ref[:, h * QK_B_PAD + NOPE:(h + 1) * QK_B_PAD] = (
            _rotate(y[:, NOPE:], cos, sa, sb) * out_scale).astype(o_ref.dtype)


def mla_q(proj_b, g_qa, wq, gq, tables, tm=512):
    n = proj_b.shape[0]
    width = NH_B * QK_B_PAD
    tab = pl.BlockSpec((tm, 128), lambda i: (i, 0))
    return pl.pallas_call(
        functools.partial(_mla_q_kernel, out_scale=QK_B ** -0.5 * LOG2E),
        grid=(n // tm,),
        in_specs=[pl.BlockSpec((tm, Q_LORA), lambda i: (i, 0)), pl.BlockSpec((1, Q_LORA), lambda i: (0, 0)),
                  pl.BlockSpec((Q_LORA, width), lambda i: (0, 0)), pl.BlockSpec((1, width), lambda i: (0, 0)),
                  tab, tab, tab],
        out_specs=pl.BlockSpec((tm, width), lambda i: (i, 0)),
        out_shape=jax.ShapeDtypeStruct((n, width), BF),
        compiler_params=_cparams(("parallel",)), name="mla_q",
    )(proj_b, g_qa.reshape(1, Q_LORA), wq, gq.reshape(1, width), *tables)


def _mla_kv_kernel(*refs, normalize, rotate):
    if rotate:
        ckv_ref, kpe_ref, gkva_ref, w_ref, gkn_ref, gkr_ref, cos_ref, sa_ref, sb_ref = refs[:9]
        outs = refs[9:]
    else:
        ckv_ref, kpe_ref, gkva_ref, w_ref, gkn_ref, gkr_ref = refs[:6]
        outs = refs[6:]
    k_ref, v_ref = outs[0], outs[1]
    c = ckv_ref[...]
    if normalize:
        c = c * lax.rsqrt(jnp.mean(c * c, axis=-1, keepdims=True) + EPS) * gkva_ref[...]
        outs[2][...] = c
    kv = jnp.dot(c.astype(BF), w_ref[...], preferred_element_type=F32)
    kpe = kpe_ref[...]
    ss_pe = jnp.sum(kpe * kpe, axis=-1, keepdims=True)
    for h in range(NH_B):
        kn = kv[:, h * (NOPE + VH):h * (NOPE + VH) + NOPE]
        r = lax.rsqrt((jnp.sum(kn * kn, axis=-1, keepdims=True) + ss_pe) * (1.0 / QK_B) + EPS)
        kr = kpe * r * gkr_ref[...]
        if rotate:
            kr = _rotate(kr, cos_ref[...], sa_ref[...], sb_ref[...])
        k_ref[:, h * QK_B_PAD:h * QK_B_PAD + NOPE] = (kn * r * gkn_ref[...]).astype(k_ref.dtype)
        k_ref[:, h * QK_B_PAD + NOPE:(h + 1) * QK_B_PAD] = kr.astype(k_ref.dtype)
        v_ref[:, h * VH:(h + 1) * VH] = kv[:, h * (NOPE + VH) + NOPE:(h + 1) * (NOPE + VH)].astype(v_ref.dtype)


def mla_kv(ckv_src, ckv_blk, kpe_src, kpe_blk, g_kva, w_kvb, g_kn, tables, *, normalize, tm=512):
    n = ckv_src.shape[0]
    rotate = tables is not None
    kw, vw = NH_B * QK_B_PAD, NH_B * VH
    in_specs = [pl.BlockSpec((tm, KV_LORA), lambda i: (i, ckv_blk)), pl.BlockSpec((tm, 128), lambda i: (i, kpe_blk)),
                pl.BlockSpec((1, KV_LORA), lambda i: (0, 0)), pl.BlockSpec((KV_LORA, NH_B * (NOPE + VH)), lambda i: (0, 0)),
                pl.BlockSpec((1, NOPE), lambda i: (0, 0)), pl.BlockSpec((1, 128), lambda i: (0, 0))]
    args = [ckv_src, kpe_src, g_kva.reshape(1, KV_LORA), w_kvb, g_kn[:NOPE].reshape(1, NOPE),
            jnp.pad(g_kn[NOPE:], (0, 128 - ROPE)).reshape(1, 128)]
    if rotate:
        in_specs += [pl.BlockSpec((tm, 128), lambda i: (i, 0))] * 3
        args += list(tables)
    out_shape = [jax.ShapeDtypeStruct((n, kw), BF), jax.ShapeDtypeStruct((n, vw), BF)]
    out_specs = [pl.BlockSpec((tm, kw), lambda i: (i, 0)), pl.BlockSpec((tm, vw), lambda i: (i, 0))]
    if normalize:
        out_shape.append(jax.ShapeDtypeStruct((n, KV_LORA), F32))
        out_specs.append(pl.BlockSpec((tm, KV_LORA), lambda i: (i, 0)))
    return pl.pallas_call(
        functools.partial(_mla_kv_kernel, normalize=normalize, rotate=rotate),
        grid=(n // tm,), in_specs=in_specs, out_specs=out_specs, out_shape=out_shape,
        compiler_params=_cparams(("parallel",)), name="mla_kv",
    )(*args)


def _rotary_tables():
    half, quarter = ROPE // 2, ROPE // 4
    inv = ROPE_BASE ** (-jnp.arange(quarter, dtype=F32) * 2.0 / half)
    t = jnp.arange(LAT_T)
    lane = np.arange(128)
    is_s2 = ((lane % half) >= quarter) & (lane < ROPE)
    is_s1 = ((lane % half) < quarter) & (lane < ROPE)
    pos = jnp.where((lane < half)[None, :], (t // GRID_W)[:, None], (t % GRID_W)[:, None]).astype(F32)
    ang = pos * inv[lane % quarter][None, :]
    live = (lane < ROPE)[None, :]
    cos = jnp.where(live, jnp.cos(ang), 1.0)
    sin = jnp.sin(ang)
    sin_a = jnp.where(is_s1[None, :], -sin, 0.0)
    sin_b = jnp.where(is_s2[None, :], sin, 0.0)

    def full(tab, fill):
        return jnp.concatenate([jnp.full((N_CTX, 128), fill, F32), jnp.tile(tab, (N_LAT_B, 1))], axis=0)

    return full(cos, 1.0), full(sin_a, 0.0), full(sin_b, 0.0)


def _natten_kernel(q_ref, k_ref, v_ref, kc_ref, vc_ref, bias_ref, o_ref, *, scale):
    rb = pl.program_id(2)
    ws = jnp.clip(rb * NA_ROWS - WIN_R // 2, 0, GRID_ROWS - NA_KROWS)
    start = pl.multiple_of(ws * GRID_W, GRID_W)
    nkeys = NA_KROWS * GRID_W
    nt = (((1,), (1,)), ((), ()))
    for h in range(NA_HEADS):
        cs = slice(h * DH_C, (h + 1) * DH_C)
        q = (q_ref[0, :, cs] * (scale * LOG2E)).astype(BF)
        kw = k_ref[0, pl.ds(start, nkeys), cs].astype(BF)
        vw = v_ref[0, pl.ds(start, nkeys), cs].astype(BF)
        s_w = lax.dot_general(q, kw, nt, preferred_element_type=F32) + bias_ref[h, 0]
        s_c = lax.dot_general(q, kc_ref[0, :, cs].astype(BF), nt, preferred_element_type=F32)
        m = jnp.maximum(jnp.max(s_w, axis=1, keepdims=True), jnp.max(s_c, axis=1, keepdims=True))
        p_w = jnp.exp2(s_w - m)
        p_c = jnp.exp2(s_c - m)
        l = jnp.sum(p_w, axis=1, keepdims=True) + jnp.sum(p_c, axis=1, keepdims=True)
        o = jnp.dot(p_w.astype(BF), vw, preferred_element_type=F32)
        o += jnp.dot(p_c.astype(BF), vc_ref[0, :, cs].astype(BF), preferred_element_type=F32)
        o_ref[0, :, cs] = (o / l).astype(o_ref.dtype)


def _natten_bias(rpb):
    n_dr, n_dc = 2 * WIN_R - 1, 2 * WIN_C - 1
    cq = np.arange(GRID_W)[:, None]
    kc = np.arange(GRID_W)[None, :]
    cs = np.clip(cq - WIN_C // 2, 0, GRID_W - WIN_C)
    col_ok = (kc >= cs) & (kc < cs + WIN_C)
    dc = np.clip(kc - cq + WIN_C - 1, 0, n_dc - 1)
    col_sel = (dc[:, :, None] == np.arange(n_dc)).astype(np.float32)
    row_sel, row_ok = [], []
    for r0 in (0, 2 * NA_ROWS, GRID_ROWS - NA_ROWS):
        ws = int(np.clip(r0 - WIN_R // 2, 0, GRID_ROWS - NA_KROWS))
        r = r0 + np.arange(NA_ROWS)[:, None]
        kr = ws + np.arange(NA_KROWS)[None, :]
        rs = np.clip(r - WIN_R // 2, 0, GRID_ROWS - WIN_R)
        row_ok.append((kr >= rs) & (kr < rs + WIN_R))
        dr = np.clip(kr - r + WIN_R - 1, 0, n_dr - 1)
        row_sel.append((dr[:, :, None] == np.arange(n_dr)).astype(np.float32))
    row_sel, row_ok = np.stack(row_sel), np.stack(row_ok)
    hi = lax.Precision.HIGHEST
    t = jnp.einsum("hrc,sijr->hsijc", rpb.astype(F32), row_sel, precision=hi)
    t = jnp.einsum("hsijc,qkc->hsiqjk", t, col_sel, precision=hi)
    ok = row_ok[:, :, None, :, None] & col_ok[None, None, :, None, :]
    t = jnp.where(ok[None], t * LOG2E, NEG)
    return t.reshape(NH_C, 3, NA_ROWS * GRID_W, NA_KROWS * GRID_W)


def natten(qkv, k_ctx, v_ctx, rpb):
    bias = _natten_bias(rpb)
    nrb = GRID_ROWS // NA_ROWS
    tq = NA_ROWS * GRID_W
    nkeys = NA_KROWS * GRID_W

    def case(rb):
        return jnp.where(rb == 0, 0, jnp.where(rb == nrb - 1, 2, 1))

    ng = NH_C // NA_HEADS
    wd = NA_HEADS * DH_C
    return pl.pallas_call(
        functools.partial(_natten_kernel, scale=DH_C ** -0.5),
        grid=(N_LAT_B, ng, nrb),
        in_specs=[
            pl.BlockSpec((1, tq, wd), lambda b, h, rb: (1 + b, rb, h)),
            pl.BlockSpec((1, LAT_T, wd), lambda b, h, rb: (1 + b, 0, ng + h)),
            pl.BlockSpec((1, LAT_T, wd), lambda b, h, rb: (1 + b, 0, 2 * ng + h)),
            pl.BlockSpec((1, PAST, wd), lambda b, h, rb: (b, 0, h)),
            pl.BlockSpec((1, PAST, wd), lambda b, h, rb: (b, 0, h)),
            pl.BlockSpec((NA_HEADS, 1, tq, nkeys), lambda b, h, rb: (h, case(rb), 0, 0)),
        ],
        out_specs=pl.BlockSpec((1, tq, wd), lambda b, h, rb: (b, rb, h)),
        out_shape=jax.ShapeDtypeStruct((N_LAT_B, LAT_T, NH_C * DH_C), BF),
        compiler_params=_cparams(("parallel", "parallel", "arbitrary")), name="natten",
    )(qkv, qkv, qkv, k_ctx, v_ctx, bias)


def _mlstm_kernel(q_ref, k_ref, v_ref, og_ref, gates_ref, c0_ref, nm0_ref, ghn_ref,
                  y_ref, cn_ref, nmn_ref, hf_sc, hb_sc, c_sc, *, nc):
    L = CHUNK
    row = lax.broadcasted_iota(jnp.int32, (L, L), 0)
    col = lax.broadcasted_iota(jnp.int32, (L, L), 1)
    eye = row == col
    nt = (((1,), (1,)), ((), ()))
    tn = (((0,), (0,)), ((), ()))

    def to_col(r):
        return jnp.sum(jnp.where(eye, jnp.broadcast_to(r, (L, L)), 0.0), axis=1, keepdims=True)

    def chunk(c, d, n, m, reverse):
        sl = pl.ds(pl.multiple_of(c * L, L), L)
        g = gates_ref[0, 0, c]
        ic_row = g[2 * d:2 * d + 1, :]
        lf_row = g[2 * d + 1:2 * d + 2, :]
        q = q_ref[0, sl, :]
        k = k_ref[0, sl, :] * (DQK_A ** -0.5)
        v = v_ref[0, sl, :].astype(BF)
        mask = (col >= row) if reverse else (col <= row)
        mask_t = (row >= col) if reverse else (row <= col)
        lf_b = jnp.broadcast_to(lf_row, (L, L))
        b_col = jnp.sum(jnp.where(mask, lf_b, 0.0), axis=1, keepdims=True)
        lf_col = jnp.sum(jnp.where(eye, lf_b, 0.0), axis=1, keepdims=True)
        b_row = jnp.sum(jnp.where(mask_t, lf_col, 0.0), axis=0, keepdims=True)
        ic_col = to_col(ic_row)
        dmat = jnp.where(mask, b_col - b_row + ic_row, NEG)
        inter = b_col + m
        m_t = jnp.maximum(inter, jnp.max(dmat, axis=1, keepdims=True))
        a = jnp.exp(inter - m_t)
        qb = q.astype(BF)
        s = lax.dot_general(qb, k.astype(BF), nt, preferred_element_type=F32) * jnp.exp(dmat - m_t)
        c_old = c_sc[d]
        num = a * jnp.dot(qb, c_old.astype(BF), preferred_element_type=F32)
        num += jnp.dot(s.astype(BF), v, preferred_element_type=F32)
        den = a * jnp.sum(q * n, axis=1, keepdims=True) + jnp.sum(s, axis=1, keepdims=True)
        h = num / jnp.maximum(jnp.abs(den), jnp.exp(-m_t))
        b_end = jnp.sum(lf_row, axis=1, keepdims=True)
        g_col = b_end - b_col + ic_col
        m_new = jnp.maximum(b_end + m, jnp.max(g_col, axis=0, keepdims=True))
        a_s = jnp.exp(b_end + m - m_new)
        kw = k * jnp.exp(g_col - m_new)
        c_sc[d] = a_s * c_old + lax.dot_general(kw.astype(BF), v, tn, preferred_element_type=F32)
        n_new = a_s * n + jnp.sum(kw, axis=0, keepdims=True)
        return h, sl, n_new, m_new

    c_sc[0] = c0_ref[0, 0, 0]
    c_sc[1] = c0_ref[0, 1, 0]
    init = (nm0_ref[0, 0, 0, 0:1, :], nm0_ref[0, 0, 0, 1:2, 0:1],
            nm0_ref[0, 1, 0, 0:1, :], nm0_ref[0, 1, 0, 1:2, 0:1])

    def body(i, carry):
        n_f, m_f, n_b, m_b = carry
        h, sl, n_f, m_f = chunk(i, 0, n_f, m_f, False)
        hf_sc[sl, :] = h
        h, sl, n_b, m_b = chunk(nc - 1 - i, 1, n_b, m_b, True)
        hb_sc[sl, :] = h
        return n_f, m_f, n_b, m_b

    n_f, m_f, n_b, m_b = lax.fori_loop(0, nc, body, init, unroll=2)

    cn_ref[0, 0, 0] = c_sc[0]
    cn_ref[0, 1, 0] = c_sc[1]
    for d, (n_d, m_d) in enumerate(((n_f, m_f), (n_b, m_b))):
        nmn_ref[0, d, 0] = jnp.concatenate(
            [n_d, jnp.broadcast_to(m_d, (1, DQK_A)), jnp.zeros((6, DQK_A), F32)], axis=0)

    rows = 256
    gain = ghn_ref[0]

    def out_body(i, _):
        sl = pl.ds(pl.multiple_of(i * rows, rows), rows)
        hs = hf_sc[sl, :] + hb_sc[sl, :]
        hn = hs * lax.rsqrt(jnp.mean(hs * hs, axis=-1, keepdims=True) + EPS) * gain
        gate = 1.0 / (1.0 + jnp.exp(-og_ref[0, sl, :]))
        y_ref[0, sl, :] = (gate * hn).astype(y_ref.dtype)
        return 0

    lax.fori_loop(0, (nc * L) // rows, out_body, 0)


def mlstm(proj, gates, c0, nm0, g_hn, *, nb, t_len, boff):
    nc = t_len // CHUNK
    kq = A_QK // DQK_A
    kv = 2 * A_QK // DV_A
    ko = (2 * A_QK + A_V) // DV_A
    return pl.pallas_call(
        functools.partial(_mlstm_kernel, nc=nc),
        grid=(nb, NH_A),
        in_specs=[
            pl.BlockSpec((1, t_len, DQK_A), lambda b, h: (boff + b, 0, h)),
            pl.BlockSpec((1, t_len, DQK_A), lambda b, h: (boff + b, 0, kq + h)),
            pl.BlockSpec((1, t_len, DV_A), lambda b, h: (boff + b, 0, kv + h)),
            pl.BlockSpec((1, t_len, DV_A), lambda b, h: (boff + b, 0, ko + h)),
            pl.BlockSpec((1, 1, nc, 4, CHUNK), lambda b, h: (b, h, 0, 0, 0)),
            pl.BlockSpec((1, 2, 1, DQK_A, DV_A), lambda b, h: (b, 0, h, 0, 0)),
            pl.BlockSpec((1, 2, 1, 8, DQK_A), lambda b, h: (b, 0, h, 0, 0)),
            pl.BlockSpec((1, 1, DV_A), lambda b, h: (h, 0, 0)),
        ],
        out_specs=[
            pl.BlockSpec((1, t_len, DV_A), lambda b, h: (b, 0, h)),
            pl.BlockSpec((1, 2, 1, DQK_A, DV_A), lambda b, h: (b, 0, h, 0, 0)),
            pl.BlockSpec((1, 2, 1, 8, DQK_A), lambda b, h: (b, 0, h, 0, 0)),
        ],
        out_shape=[
            jax.ShapeDtypeStruct((nb, t_len, A_V), BF),
            jax.ShapeDtypeStruct((nb, 2, NH_A, DQK_A, DV_A), F32),
            jax.ShapeDtypeStruct((nb, 2, NH_A, 8, DQK_A), F32),
        ],
        scratch_shapes=[pltpu.VMEM((t_len, DV_A), F32), pltpu.VMEM((t_len, DV_A), F32),
                        pltpu.VMEM((2, DQK_A, DV_A), F32)],
        compiler_params=_cparams(("parallel", "parallel")), name="mlstm",
    )(proj, proj, proj, proj, gates, c0, nm0, g_hn.reshape(NH_A, 1, DV_A))


def _w1_prep_kernel(w_ref, p_ref, o_ref):
    gw = 2 * MOE_GRP
    for c in range(w_ref.shape[2] // gw):
        blk = w_ref[0, :, c * gw:(c + 1) * gw].astype(BF)
        o_ref[0, :, c * gw:(c + 1) * gw] = jnp.dot(blk, p_ref[...], preferred_element_type=F32).astype(BF)


def _deinterleave_perm():
    gw = 2 * MOE_GRP
    j = np.arange(gw)
    src = np.where(j < MOE_GRP, 2 * j, 2 * (j - MOE_GRP) + 1)
    perm = np.zeros((gw, gw), np.float32)
    perm[src, j] = 1.0
    return jnp.asarray(perm, BF)


def w1_prep(w1):
    ne, d, n2 = w1.shape
    tn = 1024
    gw = 2 * MOE_GRP
    return pl.pallas_call(
        _w1_prep_kernel,
        grid=(ne, n2 // tn),
        in_specs=[pl.BlockSpec((1, d, tn), lambda e, j: (e, 0, j)), pl.BlockSpec((gw, gw), lambda e, j: (0, 0))],
        out_specs=pl.BlockSpec((1, d, tn), lambda e, j: (e, 0, j)),
        out_shape=jax.ShapeDtypeStruct((ne, d, n2), BF),
        compiler_params=_cparams(("parallel", "parallel")), name="w1_prep",
    )(w1, _deinterleave_perm())


def _cast_kernel(x_ref, o_ref):
    o_ref[...] = x_ref[...].astype(o_ref.dtype)


def cast_bf16(x, rows):
    ne, r, c = x.shape
    return pl.pallas_call(
        _cast_kernel,
        grid=(ne, r // rows),
        in_specs=[pl.BlockSpec((1, rows, c), lambda e, j: (e, j, 0))],
        out_specs=pl.BlockSpec((1, rows, c), lambda e, j: (e, j, 0)),
        out_shape=jax.ShapeDtypeStruct(x.shape, BF),
        compiler_params=_cparams(("parallel", "parallel")), name="cast_bf16",
    )(x)


def _moe_kernel(blk_e_ref, nused_ref, x_ref, w1_ref, b1_ref, w2_ref, b2_ref, o_ref):
    b = pl.program_id(0)
    f = pl.program_id(1)

    @pl.when(f == 0)
    def _():
        o_ref[...] = jnp.broadcast_to(b2_ref[0], o_ref.shape)

    @pl.when(b < nused_ref[0])
    def _():
        x = _unpack_bf16_pairs(x_ref[...])
        gw = 2 * MOE_GRP
        part = None
        for c in range(MOE_TF // MOE_GRP):
            hh = jnp.dot(x, w1_ref[0, :, c * gw:(c + 1) * gw], preferred_element_type=F32)
            hh = hh + b1_ref[0, :, c * gw:(c + 1) * gw]
            glu = jnp.minimum(hh[:, :MOE_GRP], SWIGLU_LIMIT)
            lin = jnp.clip(hh[:, MOE_GRP:], -SWIGLU_LIMIT, SWIGLU_LIMIT)
            act = glu * (1.0 / (1.0 + jnp.exp(-SWIGLU_ALPHA * glu))) * (lin + 1.0)
            p = jnp.dot(act.astype(BF), w2_ref[0, c * MOE_GRP:(c + 1) * MOE_GRP, :], preferred_element_type=F32)
            part = p if part is None else part + p
        o_ref[...] += part


def moe_experts(xb, blk_e, nused, w1p, b1p, w2, b2):
    nf = D_FF // MOE_TF
    d = D_MODEL

    def eidx(b, be):
        return be[b]

    def fidx(b, f, nu):
        return jnp.where(b < nu[0], f, nf - 1)

    grid_spec = pltpu.PrefetchScalarGridSpec(
        num_scalar_prefetch=2,
        grid=(MOE_NB, nf),
        in_specs=[
            pl.BlockSpec((MOE_BM, d // 2), lambda b, f, be, nu: (jnp.minimum(b, nu[0] - 1), 0)),
            pl.BlockSpec((1, d, 2 * MOE_TF), lambda b, f, be, nu: (eidx(b, be), 0, fidx(b, f, nu)),
                         pipeline_mode=pl.Buffered(MOE_WBUF)),
            pl.BlockSpec((1, 1, 2 * MOE_TF), lambda b, f, be, nu: (eidx(b, be), 0, fidx(b, f, nu))),
            pl.BlockSpec((1, MOE_TF, d), lambda b, f, be, nu: (eidx(b, be), fidx(b, f, nu), 0),
                         pipeline_mode=pl.Buffered(MOE_WBUF)),
            pl.BlockSpec((1, 1, d), lambda b, f, be, nu: (eidx(b, be), 0, 0)),
        ],
        out_specs=pl.BlockSpec((MOE_BM, d), lambda b, f, be, nu: (b, 0)),
    )
    return pl.pallas_call(
        _moe_kernel,
        grid_spec=grid_spec,
        out_shape=jax.ShapeDtypeStruct((MOE_NB * MOE_BM, d), F32),
        compiler_params=_cparams(("arbitrary", "arbitrary")), name="moe_experts",
    )(blk_e, nused, xb, w1p, b1p, w2, b2)


def moe(hdn, logits, w1, b1, w2, b2):
    n = hdn.shape[0]
    nk = n * TOP_K
    top_v, top_i = lax.top_k(logits, TOP_K)
    gate = jax.nn.softmax(top_v, axis=-1)
    e = top_i.reshape(-1).astype(jnp.int32)
    rb = 512
    onehot = (e[None, :] == jnp.arange(N_EXP, dtype=jnp.int32)[:, None]).astype(BF).reshape(N_EXP, nk // rb, rb)
    within = jnp.einsum("ebs,ts->ebt", onehot, jnp.tril(jnp.ones((rb, rb), BF)), preferred_element_type=F32)
    tot = within[:, :, -1]
    before = jnp.cumsum(tot, axis=1) - tot
    rank = jnp.sum((within + before[:, :, None]) * onehot.astype(F32), axis=0).reshape(nk).astype(jnp.int32) - 1
    counts = (before[:, -1] + tot[:, -1]).astype(jnp.int32)
    padded = (counts + MOE_BM - 1) // MOE_BM * MOE_BM
    pad_end = jnp.cumsum(padded)
    pad_start = pad_end - padded
    dest = pad_start[e] + rank
    tok = jnp.arange(nk, dtype=jnp.int32) // TOP_K
    slot_tok = (jnp.arange(MOE_NB * MOE_BM, dtype=jnp.int32) % n).at[dest].set(
        tok, unique_indices=True, mode="promise_in_bounds")
    blk_start = jnp.arange(MOE_NB, dtype=jnp.int32) * MOE_BM
    blk_e = jnp.minimum(jnp.sum((pad_end[None, :] <= blk_start[:, None]).astype(jnp.int32), axis=1), N_EXP - 1)
    nused = (pad_end[-1] // MOE_BM).astype(jnp.int32).reshape(1)
    xb = hdn.at[slot_tok].get(mode="promise_in_bounds")
    b1p = b1.reshape(N_EXP, D_FF // MOE_GRP, MOE_GRP, 2).transpose(0, 1, 3, 2).reshape(N_EXP, 1, 2 * D_FF)
    yb = moe_experts(xb, blk_e, nused, w1_prep(w1), b1p, cast_bf16(w2, 1024), b2[:, None, :])
    return yb, dest.reshape(n, TOP_K), gate


def moe_combine(yb, dest, gate, lo, hi):
    y = None
    for k in range(TOP_K):
        yk = yb.at[dest[lo:hi, k]].get(mode="promise_in_bounds") * gate[lo:hi, k:k + 1]
        y = yk if y is None else y + yk
    return y


def _pad_cols(w, width):
    return jnp.pad(w, ((0, 0), (0, width - w.shape[1])))


def _adaln(cond, ada_w, ada_b):
    a = jnp.pad(jax.nn.silu(cond), ((0, 8 - cond.shape[0]), (0, 0)))
    mod = matmul(a, ada_w, bias=ada_b, tm=8, tn=1024)
    return mod[:cond.shape[0]].reshape(cond.shape[0], 6, D_MODEL)


def _layer0_mixer(x, mod, g_mix, w_in, b_gate, g_qa, w_qb, g_kva, w_kvb, g_qn, g_kn, g_hn, w_out, cache):
    st_c, st_n, st_m, c_ckv, c_kpe = cache
    o_gate = 2 * A_QK + 2 * A_V
    o_cq = o_gate + N_GATE
    w_gate = _pad_cols(w_in[:, o_gate:o_cq], AUX_W)
    hdn, gates = norm_mod(x, g_mix, mod, 0, aux_w=w_gate, aux_b=jnp.pad(b_gate, (0, AUX_W - N_GATE)))
    proj = matmul(hdn, w_in[:, :o_gate].astype(BF))
    w_b = _pad_cols(w_in[:, o_cq:], MLA_IN_W).astype(BF)
    proj_b = matmul(hdn, w_b, tn=MLA_IN_W)

    gt = GATE_CAP * jnp.tanh(gates[:, :N_GATE] / GATE_CAP)
    gt = gt.reshape(N_TOK, 4, NH_A)
    gt = jnp.stack([gt[:, 0], jax.nn.log_sigmoid(gt[:, 1]), gt[:, 2], jax.nn.log_sigmoid(gt[:, 3])], axis=1)

    def gate_layout(g, nb, t_len):
        g = g.reshape(nb, t_len // CHUNK, CHUNK, 4, NH_A)
        return g.transpose(0, 4, 1, 3, 2)

    zc = jnp.zeros((N_CTX_B, 2, NH_A, DQK_A, DV_A), F32)
    znm = jnp.zeros((N_CTX_B, 2, NH_A, 8, DQK_A), F32)
    y_ctx, new_c, new_nm = mlstm(proj.reshape(N_TOK // CTX_T, CTX_T, -1), gate_layout(gt[:N_CTX], N_CTX_B, CTX_T),
                                 zc, znm, g_hn, nb=N_CTX_B, t_len=CTX_T, boff=0)
    nm_lat = jnp.concatenate([st_n[:, :, :, None, :],
                              jnp.broadcast_to(st_m[:, :, :, None, None], (N_LAT_B, 2, NH_A, 1, DQK_A)),
                              jnp.zeros((N_LAT_B, 2, NH_A, 6, DQK_A), F32)], axis=3)
    y_lat, _, _ = mlstm(proj.reshape(N_TOK // LAT_T, LAT_T, -1), gate_layout(gt[N_CTX:], N_LAT_B, LAT_T),
                        st_c, nm_lat, g_hn, nb=N_LAT_B, t_len=LAT_T, boff=1)
    y_a = jnp.concatenate([y_ctx.reshape(N_CTX, A_V), y_lat.reshape(N_LAT, A_V)], axis=0)
    new_n = new_nm[:, :, :, 0, :]
    new_m = new_nm[:, :, :, 1, 0]

    wq = w_qb.reshape(Q_LORA, NH_B, QK_B)
    wq = jnp.pad(wq, ((0, 0), (0, 0), (0, QK_B_PAD - QK_B))).reshape(Q_LORA, NH_B * QK_B_PAD).astype(BF)
    gq = jnp.tile(jnp.pad(g_qn, (0, QK_B_PAD - QK_B)), NH_B)
    tables = _rotary_tables()
    w_kvb_b = w_kvb.astype(BF)
    q = mla_q(proj_b, g_qa, wq, gq, tables)
    k_new, v_new, ckv_n = mla_kv(proj_b, 1, proj_b, (Q_LORA + KV_LORA) // 128, g_kva, w_kvb_b, g_kn, tables,
                                 normalize=True)
    k_old, v_old = mla_kv(c_ckv.reshape(N_LAT_B * PAST, KV_LORA), 0,
                          _pad_cols(c_kpe.reshape(N_LAT_B * PAST, ROPE), 128), 0, g_kva, w_kvb_b, g_kn, None,
                          normalize=False)
    kpe = proj_b[:N_CTX, Q_LORA + KV_LORA:Q_LORA + KV_LORA + ROPE]
    ident = lambda b, h: (b, h)
    lat = lambda b, h: (1 + b, h)
    kw, vw = NH_B * QK_B_PAD, NH_B * VH
    y_b_ctx = attention(
        q.reshape(N_TOK // CTX_T, CTX_T, kw), ident,
        [(k_new.reshape(N_TOK // CTX_T, CTX_T, kw), ident, v_new.reshape(N_TOK // CTX_T, CTX_T, vw), ident, CTX_T)],
        nb=N_CTX_B, nh=NH_B, tq_len=CTX_T, dk=QK_B_PAD, dv=VH, tq=CTX_T, q_scale=None)
    y_b_lat = attention(
        q.reshape(N_TOK // LAT_T, LAT_T, kw), lat,
        [(k_new.reshape(N_TOK // LAT_T, LAT_T, kw), lat, v_new.reshape(N_TOK // LAT_T, LAT_T, vw), lat, LAT_T),
         (k_old.reshape(N_LAT_B, PAST, kw), ident, v_old.reshape(N_LAT_B, PAST, vw), ident, PAST)],
        nb=N_LAT_B, nh=NH_B, tq_len=LAT_T, dk=QK_B_PAD, dv=VH, tq=256, q_scale=None)
    y_b = jnp.concatenate([y_b_ctx.reshape(N_CTX, -1), y_b_lat.reshape(N_LAT, -1)], axis=0)

    y = jnp.concatenate([y_a, y_b], axis=1)
    x = matmul(y, w_out.astype(BF), resid=x, mod=mod, gate_row=2)
    new = (new_c, new_n, new_m, ckv_n[:N_CTX].reshape(N_CTX_B, CTX_T, KV_LORA), kpe.reshape(N_CTX_B, CTX_T, ROPE))
    return x, new


def _layer1_mixer(x, mod, g_mix, w_qkv, g_qn, g_kn, rpb, w_out, cache):
    c_k, c_v = cache
    hd = NH_C * DH_C
    hdn = norm_mod(x, g_mix, mod, 0)
    gain = jnp.concatenate([jnp.tile(g_qn, NH_C), jnp.tile(g_kn, NH_C), jnp.ones((hd,), F32)])
    qkv = matmul(hdn, w_qkv.astype(BF), gain=gain, norm_group=DH_C, norm_div=float(DH_C), n_norm_cols=2 * hd)
    new_k = qkv[:N_CTX, hd:2 * hd].reshape(N_CTX_B, CTX_T, NH_C, DH_C)
    new_v = qkv[:N_CTX, 2 * hd:].reshape(N_CTX_B, CTX_T, NH_C, DH_C)
    qkv_c = qkv.reshape(N_TOK // CTX_T, CTX_T, 3 * hd)
    o_ctx = attention(
        qkv_c, lambda b, h: (b, h),
        [(qkv_c, lambda b, h: (b, NH_C + h), qkv_c, lambda b, h: (b, 2 * NH_C + h), CTX_T)],
        nb=N_CTX_B, nh=NH_C, tq_len=CTX_T, dk=DH_C, dv=DH_C, tq=CTX_T, q_scale=DH_C ** -0.5)
    o_lat = natten(qkv.reshape(N_TOK // LAT_T, LAT_T, 3 * hd), c_k.reshape(N_LAT_B, PAST, hd),
                   c_v.reshape(N_LAT_B, PAST, hd), rpb)
    o = jnp.concatenate([o_ctx.reshape(N_CTX, hd), o_lat.reshape(N_LAT, hd)], axis=0)
    x = matmul(o, w_out.astype(BF), resid=x, mod=mod, gate_row=2)
    return x, (new_k, new_v)


def _moe_sublayer(x, mod, g_moe, w_r, b_r, w1, b1, w2, b2, split=False):
    hdn, logits = norm_mod(x, g_moe, mod, 1, aux_w=_pad_cols(w_r, AUX_W), aux_b=jnp.pad(b_r, (0, AUX_W - N_EXP)),
                           packed=True)
    yb, dest, gate = moe(hdn, logits[:, :N_EXP], w1, b1, w2, b2)
    ga = mod[:, 5, :]
    if not split:
        ga_rows = jnp.repeat(ga, N_CTX, axis=0, total_repeat_length=N_TOK)
        return x + ga_rows * moe_combine(yb, dest, gate, 0, N_TOK)
    y_ctx = x[:N_CTX] + ga[0][None, :] * moe_combine(yb, dest, gate, 0, N_CTX)
    y_lat = x[N_CTX:].reshape(N_LAT_B, LAT_T, D_MODEL) + ga[1:, None, :] * moe_combine(
        yb, dest, gate, N_CTX, N_TOK).reshape(N_LAT_B, LAT_T, D_MODEL)
    return y_ctx, y_lat.reshape(N_LAT, D_MODEL)


def kernel(x_prompt, x_sample, state_l0_mlstm_C, state_l0_mlstm_n, state_l0_mlstm_m, cache_l0_mla_ckv, cache_l0_mla_kpe, cache_l1_na_k, cache_l1_na_v, c, c_ctx, l0_g_mix, l0_g_moe, l0_ada_w, l0_ada_b, l0_w_in, l0_b_gate, l0_g_qa, l0_w_qb, l0_g_kva, l0_w_kvb, l0_g_qn, l0_g_kn, l0_g_hn, l0_w_out, l0_w_router, l0_b_router, l0_w1, l0_b1, l0_w2, l0_b2, l1_g_mix, l1_g_moe, l1_ada_w, l1_ada_b, l1_w_qkv, l1_g_qn, l1_g_kn, l1_rpb, l1_w_out, l1_w_router, l1_b_router, l1_w1, l1_b1, l1_w2, l1_b2):
    x = jnp.concatenate([x_prompt.reshape(N_CTX, D_MODEL), x_sample.reshape(N_LAT, D_MODEL)], axis=0)
    cond = jnp.concatenate([c_ctx[None, :], c], axis=0)

    mod0 = _adaln(cond, l0_ada_w, l0_ada_b)
    x, (new_c, new_n, new_m, new_ckv, new_kpe) = _layer0_mixer(
        x, mod0, l0_g_mix, l0_w_in, l0_b_gate, l0_g_qa, l0_w_qb, l0_g_kva, l0_w_kvb, l0_g_qn, l0_g_kn, l0_g_hn,
        l0_w_out, (state_l0_mlstm_C, state_l0_mlstm_n, state_l0_mlstm_m, cache_l0_mla_ckv, cache_l0_mla_kpe))
    x = _moe_sublayer(x, mod0, l0_g_moe, l0_w_router, l0_b_router, l0_w1, l0_b1, l0_w2, l0_b2)

    mod1 = _adaln(cond, l1_ada_w, l1_ada_b)
    x, (new_k, new_v) = _layer1_mixer(x, mod1, l1_g_mix, l1_w_qkv, l1_g_qn, l1_g_kn, l1_rpb, l1_w_out,
                                      (cache_l1_na_k, cache_l1_na_v))
    y_ctx, y_lat = _moe_sublayer(x, mod1, l1_g_moe, l1_w_router, l1_b_router, l1_w1, l1_b1, l1_w2, l1_b2, split=True)

    y_prompt = y_ctx.reshape(N_CTX_B, CTX_T, D_MODEL)
    y_sample = y_lat.reshape(N_LAT_B, LAT_T, D_MODEL)
    return (y_prompt, y_sample, new_c, new_n, new_m, new_ckv, new_kpe, new_k, new_v)
```

```python
import functools

import numpy as np
import jax
import jax.numpy as jnp
from jax import lax
from jax.experimental import pallas as pl
from jax.experimental.pallas import tpu as pltpu

BF = jnp.bfloat16
F32 = jnp.float32

D_MODEL = 2048
N_CTX_B, CTX_T = 16, 256
N_LAT_B, LAT_T = 4, 4096
N_CTX = N_CTX_B * CTX_T
N_LAT = N_LAT_B * LAT_T
N_TOK = N_CTX + N_LAT
PAST = 512
GRID_W = 64
GRID_ROWS = LAT_T // GRID_W
EPS = 1e-6
NEG = -1e30

NH_A, DQK_A, DV_A = 4, 128, 256
A_QK, A_V = NH_A * DQK_A, NH_A * DV_A
N_GATE = 4 * NH_A
CHUNK = 64
GATE_CAP = 15.0
NH_B, Q_LORA, KV_LORA, NOPE, ROPE, VH = 8, 512, 512, 128, 64, 128
QK_B = NOPE + ROPE
QK_B_PAD = 256
ROPE_BASE = 10000.0
NH_C, DH_C = 16, 128
WIN_R, WIN_C = 8, 16
NA_ROWS = 4
NA_KROWS = 12
NA_HEADS = 4
LOG2E = 1.4426950408889634
N_EXP, TOP_K, D_FF = 32, 4, 2048
SWIGLU_ALPHA, SWIGLU_LIMIT = 1.702, 7.0
MOE_BM = 512
MOE_TF = 1024
MOE_GRP = 256
MOE_NB = (N_TOK * TOP_K + N_EXP * (MOE_BM - 1) + MOE_BM - 1) // MOE_BM
AUX_W = 128
MLA_IN_W = 1280

VMEM_LIMIT = 56 * 1024 * 1024


def _cparams(sem):
    return pltpu.CompilerParams(dimension_semantics=sem, vmem_limit_bytes=VMEM_LIMIT)


def _split_bf16(x):
    hi = x.astype(BF)
    lo = (x - hi.astype(F32)).astype(BF)
    return hi, lo


def _pack_bf16_pairs(h):
    c = h.shape[1] // 2
    hb = h.astype(BF).astype(F32)
    hi = lax.bitcast_convert_type(hb[:, :c], jnp.uint32)
    lo = lax.bitcast_convert_type(hb[:, c:], jnp.uint32)
    return hi | (lo >> 16)


def _unpack_bf16_pairs(w):
    hi = lax.bitcast_convert_type(w & jnp.uint32(0xFFFF0000), F32).astype(BF)
    lo = lax.bitcast_convert_type(w << 16, F32).astype(BF)
    return jnp.concatenate([hi, lo], axis=1)


def _norm_mod_kernel(*refs, which, has_aux, packed):
    if has_aux:
        x_ref, g_ref, mod_ref, w_ref, b_ref, o_ref, aux_ref = refs
    else:
        x_ref, g_ref, mod_ref, o_ref = refs
    x = x_ref[...]
    ms = jnp.mean(x * x, axis=-1, keepdims=True)
    y = x * lax.rsqrt(ms + EPS) * g_ref[...]
    sh = mod_ref[0, 3 * which:3 * which + 1, :]
    sc = mod_ref[0, 3 * which + 1:3 * which + 2, :]
    h = y * (1.0 + sc) + sh
    o_ref[...] = _pack_bf16_pairs(h) if packed else h.astype(BF)
    if has_aux:
        h_hi, h_lo = _split_bf16(h)
        w_hi, w_lo = _split_bf16(w_ref[...])
        acc = jnp.dot(h_hi, w_hi, preferred_element_type=F32)
        acc += jnp.dot(h_hi, w_lo, preferred_element_type=F32)
        acc += jnp.dot(h_lo, w_hi, preferred_element_type=F32)
        aux_ref[...] = acc + b_ref[...]


def norm_mod(x, g, mod, which, aux_w=None, aux_b=None, packed=False, tm=512):
    n, d = x.shape
    has_aux = aux_w is not None
    in_specs = [
        pl.BlockSpec((tm, d), lambda i: (i, 0)),
        pl.BlockSpec((1, d), lambda i: (0, 0)),
        pl.BlockSpec((1, 6, d), lambda i: ((i * tm) // N_CTX, 0, 0)),
    ]
    args = [x, g.reshape(1, d), mod]
    od, odt = (d // 2, jnp.uint32) if packed else (d, BF)
    out_shape = [jax.ShapeDtypeStruct((n, od), odt)]
    out_specs = [pl.BlockSpec((tm, od), lambda i: (i, 0))]
    if has_aux:
        in_specs += [pl.BlockSpec((d, AUX_W), lambda i: (0, 0)), pl.BlockSpec((1, AUX_W), lambda i: (0, 0))]
        args += [aux_w, aux_b.reshape(1, AUX_W)]
        out_shape.append(jax.ShapeDtypeStruct((n, AUX_W), F32))
        out_specs.append(pl.BlockSpec((tm, AUX_W), lambda i: (i, 0)))
    res = pl.pallas_call(
        functools.partial(_norm_mod_kernel, which=which, has_aux=has_aux, packed=packed),
        grid=(n // tm,), in_specs=in_specs, out_specs=out_specs, out_shape=out_shape,
        compiler_params=_cparams(("parallel",)), name="norm_mod",
    )(*args)
    return res if has_aux else res[0]


def _group_rmsnorm(acc, gain, group, div):
    parts = []
    for c in range(acc.shape[1] // group):
        blk = acc[:, c * group:(c + 1) * group]
        ms = jnp.sum(blk * blk, axis=-1, keepdims=True) * (1.0 / div)
        parts.append(blk * lax.rsqrt(ms + EPS) * gain[:, c * group:(c + 1) * group])
    return parts[0] if len(parts) == 1 else jnp.concatenate(parts, axis=-1)


def _mm_kernel(*refs, has_bias, gate_row, norm_group, norm_div, n_norm_tiles):
    a_ref, w_ref = refs[0], refs[1]
    pos = 2
    if has_bias:
        b_ref = refs[pos]; pos += 1
    if gate_row is not None:
        x_ref, mod_ref = refs[pos], refs[pos + 1]; pos += 2
    if norm_group:
        gain_ref = refs[pos]; pos += 1
    o_ref = refs[pos]

    acc = jnp.dot(a_ref[...].astype(BF), w_ref[...].astype(BF), preferred_element_type=F32)
    if has_bias:
        acc = acc + b_ref[...]
    if gate_row is not None:
        acc = x_ref[...] + mod_ref[0, gate_row:gate_row + 1, :] * acc
    if norm_group:
        j = pl.program_id(1)

        @pl.when(j < n_norm_tiles)
        def _():
            o_ref[...] = _group_rmsnorm(acc, gain_ref[...], norm_group, norm_div).astype(o_ref.dtype)

        @pl.when(j >= n_norm_tiles)
        def _():
            o_ref[...] = acc.astype(o_ref.dtype)
    else:
        o_ref[...] = acc.astype(o_ref.dtype)


def matmul(a, w, *, bias=None, resid=None, mod=None, gate_row=None, gain=None, norm_group=0,
           norm_div=1.0, n_norm_cols=None, out_dtype=F32, tm=1024, tn=512):
    m, k = a.shape
    n = w.shape[1]
    tm = min(tm, m)
    tn = min(tn, n)
    assert m % tm == 0 and n % tn == 0
    in_specs = [pl.BlockSpec((tm, k), lambda i, j: (i, 0)), pl.BlockSpec((k, tn), lambda i, j: (0, j))]
    args = [a, w]
    if bias is not None:
        in_specs.append(pl.BlockSpec((1, tn), lambda i, j: (0, j)))
        args.append(bias.reshape(1, n))
    if gate_row is not None:
        in_specs.append(pl.BlockSpec((tm, tn), lambda i, j: (i, j)))
        in_specs.append(pl.BlockSpec((1, 6, tn), lambda i, j: ((i * tm) // N_CTX, 0, j)))
        args += [resid, mod]
    n_norm_tiles = 0
    if norm_group:
        n_norm_cols = n if n_norm_cols is None else n_norm_cols
        assert n_norm_cols % tn == 0 and tn % norm_group == 0
        n_norm_tiles = n_norm_cols // tn
        in_specs.append(pl.BlockSpec((1, tn), lambda i, j: (0, j)))
        args.append(gain.reshape(1, n))
    return pl.pallas_call(
        functools.partial(_mm_kernel, has_bias=bias is not None, gate_row=gate_row, norm_group=norm_group,
                          norm_div=norm_div, n_norm_tiles=n_norm_tiles),
        grid=(m // tm, n // tn), in_specs=in_specs,
        out_specs=pl.BlockSpec((tm, tn), lambda i, j: (i, j)),
        out_shape=jax.ShapeDtypeStruct((m, n), out_dtype),
        compiler_params=_cparams(("parallel", "parallel")), name="matmul",
    )(*args)


ATTN_CHUNK = 512


def _attn_kernel(*refs, q_scale, n_src):
    q_ref, o_ref = refs[0], refs[-1]
    q = q_ref[0]
    if q_scale is not None:
        q = q * (q_scale * LOG2E)
    q = q.astype(BF)
    nt = (((1,), (1,)), ((), ()))
    scores, values = [], []
    for i in range(n_src):
        k_ref, v_ref = refs[1 + 2 * i], refs[2 + 2 * i]
        tk = k_ref.shape[1]
        ch = min(ATTN_CHUNK, tk)
        for c in range(tk // ch):
            k = k_ref[0, c * ch:(c + 1) * ch, :].astype(BF)
            scores.append(lax.dot_general(q, k, nt, preferred_element_type=F32))
            values.append((v_ref, c * ch, ch))
    m = functools.reduce(jnp.maximum, [jnp.max(s, axis=1, keepdims=True) for s in scores])
    l, acc = None, None
    for s, (v_ref, start, ch) in zip(scores, values):
        p = jnp.exp2(s - m)
        ps = jnp.sum(p, axis=1, keepdims=True)
        pv = jnp.dot(p.astype(BF), v_ref[0, start:start + ch, :].astype(BF), preferred_element_type=F32)
        l = ps if l is None else l + ps
        acc = pv if acc is None else acc + pv
    o_ref[0] = (acc / l).astype(o_ref.dtype)


def attention(q, q_map, kv_sources, *, nb, nh, tq_len, dk, dv, tq, q_scale):
    in_specs = [pl.BlockSpec((1, tq, dk), lambda b, h, qi: (q_map(b, h)[0], qi, q_map(b, h)[1]))]
    args = [q]
    for k, k_map, v, v_map, n_keys in kv_sources:
        in_specs.append(pl.BlockSpec((1, n_keys, dk), lambda b, h, qi, f=k_map: (f(b, h)[0], 0, f(b, h)[1])))
        in_specs.append(pl.BlockSpec((1, n_keys, dv), lambda b, h, qi, f=v_map: (f(b, h)[0], 0, f(b, h)[1])))
        args += [k, v]
    return pl.pallas_call(
        functools.partial(_attn_kernel, q_scale=q_scale, n_src=len(kv_sources)),
        grid=(nb, nh, tq_len // tq), in_specs=in_specs,
        out_specs=pl.BlockSpec((1, tq, dv), lambda b, h, qi: (b, qi, h)),
        out_shape=jax.ShapeDtypeStruct((nb, tq_len, nh * dv), BF),
        compiler_params=_cparams(("parallel", "parallel", "arbitrary")), name="attention",
    )(*args)


def _rotate(x, cos, sin_a, sin_b):
    quarter = ROPE // 4
    return x * cos + pltpu.roll(x, 128 - quarter, 1) * sin_a + pltpu.roll(x, quarter, 1) * sin_b


def _mla_q_kernel(cq_ref, gqa_ref, wq_ref, gq_ref, cos_ref, sa_ref, sb_ref, o_ref, *, out_scale):
    cq = cq_ref[...]
    cqn = cq * lax.rsqrt(jnp.mean(cq * cq, axis=-1, keepdims=True) + EPS) * gqa_ref[...]
    q = jnp.dot(cqn.astype(BF), wq_ref[...], preferred_element_type=F32)
    cos, sa, sb = cos_ref[...], sa_ref[...], sb_ref[...]
    for h in range(NH_B):
        blk = q[:, h * QK_B_PAD:(h + 1) * QK_B_PAD]
        r = lax.rsqrt(jnp.sum(blk * blk, axis=-1, keepdims=True) * (1.0 / QK_B) + EPS)
        y = blk * r * gq_ref[:, h * QK_B_PAD:(h + 1) * QK_B_PAD]
        o_ref[:, h * QK_B_PAD:h * QK_B_PAD + NOPE] = (y[:, :NOPE] * out_scale).astype(o_ref.dtype)
        o_ref[:, h * QK_B_PAD + NOPE:(h + 1) * QK_B_PAD] = (
            _rotate(y[:, NOPE:], cos, sa, sb) * out_scale).astype(o_ref.dtype)


def mla_q(proj_b, g_qa, wq, gq, tables, tm=512):
    n = proj_b.shape[0]
    width = NH_B * QK_B_PAD
    tab = pl.BlockSpec((tm, 128), lambda i: (i, 0))
    return pl.pallas_call(
        functools.partial(_mla_q_kernel, out_scale=QK_B ** -0.5 * LOG2E),
        grid=(n // tm,),
        in_specs=[pl.BlockSpec((tm, Q_LORA), lambda i: (i, 0)), pl.BlockSpec((1, Q_LORA), lambda i: (0, 0)),
                  pl.BlockSpec((Q_LORA, width), lambda i: (0, 0)), pl.BlockSpec((1, width), lambda i: (0, 0)),
                  tab, tab, tab],
        out_specs=pl.BlockSpec((tm, width), lambda i: (i, 0)),
        out_shape=jax.ShapeDtypeStruct((n, width), BF),
        compiler_params=_cparams(("parallel",)), name="mla_q",
    )(proj_b, g_qa.reshape(1, Q_LORA), wq, gq.reshape(1, width), *tables)


def _mla_kv_kernel(*refs, normalize, rotate):
    if rotate:
        ckv_ref, kpe_ref, gkva_ref, w_ref, gkn_ref, gkr_ref, cos_ref, sa_ref, sb_ref = refs[:9]
        outs = refs[9:]
    else:
        ckv_ref, kpe_ref, gkva_ref, w_ref, gkn_ref, gkr_ref = refs[:6]
        outs = refs[6:]
    k_ref, v_ref = outs[0], outs[1]
    c = ckv_ref[...]
    if normalize:
        c = c * lax.rsqrt(jnp.mean(c * c, axis=-1, keepdims=True) + EPS) * gkva_ref[...]
        outs[2][...] = c
    kv = jnp.dot(c.astype(BF), w_ref[...], preferred_element_type=F32)
    kpe = kpe_ref[...]
    ss_pe = jnp.sum(kpe * kpe, axis=-1, keepdims=True)
    for h in range(NH_B):
        kn = kv[:, h * (NOPE + VH):h * (NOPE + VH) + NOPE]
        r = lax.rsqrt((jnp.sum(kn * kn, axis=-1, keepdims=True) + ss_pe) * (1.0 / QK_B) + EPS)
        kr = kpe * r * gkr_ref[...]
        if rotate:
            kr = _rotate(kr, cos_ref[...], sa_ref[...], sb_ref[...])
        k_ref[:, h * QK_B_PAD:h * QK_B_PAD + NOPE] = (kn * r * gkn_ref[...]).astype(k_ref.dtype)
        k_ref[:, h * QK_B_PAD + NOPE:(h + 1) * QK_B_PAD] = kr.astype(k_ref.dtype)
        v_ref[:, h * VH:(h + 1) * VH] = kv[:, h * (NOPE + VH) + NOPE:(h + 1) * (NOPE + VH)].astype(v_ref.dtype)


def mla_kv(ckv_src, ckv_blk, kpe_src, kpe_blk, g_kva, w_kvb, g_kn, tables, *, normalize, tm=512):
    n = ckv_src.shape[0]
    rotate = tables is not None
    kw, vw = NH_B * QK_B_PAD, NH_B * VH
    in_specs = [pl.BlockSpec((tm, KV_LORA), lambda i: (i, ckv_blk)), pl.BlockSpec((tm, 128), lambda i: (i, kpe_blk)),
                pl.BlockSpec((1, KV_LORA), lambda i: (0, 0)), pl.BlockSpec((KV_LORA, NH_B * (NOPE + VH)), lambda i: (0, 0)),
                pl.BlockSpec((1, NOPE), lambda i: (0, 0)), pl.BlockSpec((1, 128), lambda i: (0, 0))]
    args = [ckv_src, kpe_src, g_kva.reshape(1, KV_LORA), w_kvb, g_kn[:NOPE].reshape(1, NOPE),
            jnp.pad(g_kn[NOPE:], (0, 128 - ROPE)).reshape(1, 128)]
    if rotate:
        in_specs += [pl.BlockSpec((tm, 128), lambda i: (i, 0))] * 3
        args += list(tables)
    out_shape = [jax.ShapeDtypeStruct((n, kw), BF), jax.ShapeDtypeStruct((n, vw), BF)]
    out_specs = [pl.BlockSpec((tm, kw), lambda i: (i, 0)), pl.BlockSpec((tm, vw), lambda i: (i, 0))]
    if normalize:
        out_shape.append(jax.ShapeDtypeStruct((n, KV_LORA), F32))
        out_specs.append(pl.BlockSpec((tm, KV_LORA), lambda i: (i, 0)))
    return pl.pallas_call(
        functools.partial(_mla_kv_kernel, normalize=normalize, rotate=rotate),
        grid=(n // tm,), in_specs=in_specs, out_specs=out_specs, out_shape=out_shape,
        compiler_params=_cparams(("parallel",)), name="mla_kv",
    )(*args)


def _rotary_tables():
    half, quarter = ROPE // 2, ROPE // 4
    inv = ROPE_BASE ** (-jnp.arange(quarter, dtype=F32) * 2.0 / half)
    t = jnp.arange(LAT_T)
    lane = np.arange(128)
    is_s2 = ((lane % half) >= quarter) & (lane < ROPE)
    is_s1 = ((lane % half) < quarter) & (lane < ROPE)
    pos = jnp.where((lane < half)[None, :], (t // GRID_W)[:, None], (t % GRID_W)[:, None]).astype(F32)
    ang = pos * inv[lane % quarter][None, :]
    live = (lane < ROPE)[None, :]
    cos = jnp.where(live, jnp.cos(ang), 1.0)
    sin = jnp.sin(ang)
    sin_a = jnp.where(is_s1[None, :], -sin, 0.0)
    sin_b = jnp.where(is_s2[None, :], sin, 0.0)

    def full(tab, fill):
        return jnp.concatenate([jnp.full((N_CTX, 128), fill, F32), jnp.tile(tab, (N_LAT_B, 1))], axis=0)

    return full(cos, 1.0), full(sin_a, 0.0), full(sin_b, 0.0)


def _natten_kernel(q_ref, k_ref, v_ref, kc_ref, vc_ref, bias_ref, o_ref, *, scale):
    rb = pl.program_id(2)
    ws = jnp.clip(rb * NA_ROWS - WIN_R // 2, 0, GRID_ROWS - NA_KROWS)
    start = pl.multiple_of(ws * GRID_W, GRID_W)
    nkeys = NA_KROWS * GRID_W
    nt = (((1,), (1,)), ((), ()))
    for h in range(NA_HEADS):
        cs = slice(h * DH_C, (h + 1) * DH_C)
        q = (q_ref[0, :, cs] * (scale * LOG2E)).astype(BF)
        kw = k_ref[0, pl.ds(start, nkeys), cs].astype(BF)
        vw = v_ref[0, pl.ds(start, nkeys), cs].astype(BF)
        s_w = lax.dot_general(q, kw, nt, preferred_element_type=F32) + bias_ref[h, 0]
        s_c = lax.dot_general(q, kc_ref[0, :, cs].astype(BF), nt, preferred_element_type=F32)
        m = jnp.maximum(jnp.max(s_w, axis=1, keepdims=True), jnp.max(s_c, axis=1, keepdims=True))
        p_w = jnp.exp2(s_w - m)
        p_c = jnp.exp2(s_c - m)
        l = jnp.sum(p_w, axis=1, keepdims=True) + jnp.sum(p_c, axis=1, keepdims=True)
        o = jnp.dot(p_w.astype(BF), vw, preferred_element_type=F32)
        o += jnp.dot(p_c.astype(BF), vc_ref[0, :, cs].astype(BF), preferred_element_type=F32)
        o_ref[0, :, cs] = (o / l).astype(o_ref.dtype)


def _natten_bias(rpb):
    n_dr, n_dc = 2 * WIN_R - 1, 2 * WIN_C - 1
    cq = np.arange(GRID_W)[:, None]
    kc = np.arange(GRID_W)[None, :]
    cs = np.clip(cq - WIN_C // 2, 0, GRID_W - WIN_C)
    col_ok = (kc >= cs) & (kc < cs + WIN_C)
    dc = np.clip(kc - cq + WIN_C - 1, 0, n_dc - 1)
    col_sel = (dc[:, :, None] == np.arange(n_dc)).astype(np.float32)
    row_sel, row_ok = [], []
    for r0 in (0, 2 * NA_ROWS, GRID_ROWS - NA_ROWS):
        ws = int(np.clip(r0 - WIN_R // 2, 0, GRID_ROWS - NA_KROWS))
        r = r0 + np.arange(NA_ROWS)[:, None]
        kr = ws + np.arange(NA_KROWS)[None, :]
        rs = np.clip(r - WIN_R // 2, 0, GRID_ROWS - WIN_R)
        row_ok.append((kr >= rs) & (kr < rs + WIN_R))
        dr = np.clip(kr - r + WIN_R - 1, 0, n_dr - 1)
        row_sel.append((dr[:, :, None] == np.arange(n_dr)).astype(np.float32))
    row_sel, row_ok = np.stack(row_sel), np.stack(row_ok)
    hi = lax.Precision.HIGHEST
    t = jnp.einsum("hrc,sijr->hsijc", rpb.astype(F32), row_sel, precision=hi)
    t = jnp.einsum("hsijc,qkc->hsiqjk", t, col_sel, precision=hi)
    ok = row_ok[:, :, None, :, None] & col_ok[None, None, :, None, :]
    t = jnp.where(ok[None], t * LOG2E, NEG)
    return t.reshape(NH_C, 3, NA_ROWS * GRID_W, NA_KROWS * GRID_W)


def natten(qkv, k_ctx, v_ctx, rpb):
    bias = _natten_bias(rpb)
    nrb = GRID_ROWS // NA_ROWS
    tq = NA_ROWS * GRID_W
    nkeys = NA_KROWS * GRID_W

    def case(rb):
        return jnp.where(rb == 0, 0, jnp.where(rb == nrb - 1, 2, 1))

    ng = NH_C // NA_HEADS
    wd = NA_HEADS * DH_C
    return pl.pallas_call(
        functools.partial(_natten_kernel, scale=DH_C ** -0.5),
        grid=(N_LAT_B, ng, nrb),
        in_specs=[
            pl.BlockSpec((1, tq, wd), lambda b, h, rb: (1 + b, rb, h)),
            pl.BlockSpec((1, LAT_T, wd), lambda b, h, rb: (1 + b, 0, ng + h)),
            pl.BlockSpec((1, LAT_T, wd), lambda b, h, rb: (1 + b, 0, 2 * ng + h)),
            pl.BlockSpec((1, PAST, wd), lambda b, h, rb: (b, 0, h)),
            pl.BlockSpec((1, PAST, wd), lambda b, h, rb: (b, 0, h)),
            pl.BlockSpec((NA_HEADS, 1, tq, nkeys), lambda b, h, rb: (h, case(rb), 0, 0)),
        ],
        out_specs=pl.BlockSpec((1, tq, wd), lambda b, h, rb: (b, rb, h)),
        out_shape=jax.ShapeDtypeStruct((N_LAT_B, LAT_T, NH_C * DH_C), BF),
        compiler_params=_cparams(("parallel", "parallel", "arbitrary")), name="natten",
    )(qkv, qkv, qkv, k_ctx, v_ctx, bias)


def _mlstm_kernel(q_ref, k_ref, v_ref, og_ref, gates_ref, c0_ref, nm0_ref, ghn_ref,
                  y_ref, cn_ref, nmn_ref, hf_sc, hb_sc, c_sc, *, nc):
    L = CHUNK
    row = lax.broadcasted_iota(jnp.int32, (L, L), 0)
    col = lax.broadcasted_iota(jnp.int32, (L, L), 1)
    eye = row == col
    nt = (((1,), (1,)), ((), ()))
    tn = (((0,), (0,)), ((), ()))

    def to_col(r):
        return jnp.sum(jnp.where(eye, jnp.broadcast_to(r, (L, L)), 0.0), axis=1, keepdims=True)

    def chunk(c, d, n, m, reverse):
        sl = pl.ds(pl.multiple_of(c * L, L), L)
        g = gates_ref[0, 0, c]
        ic_row = g[2 * d:2 * d + 1, :]
        lf_row = g[2 * d + 1:2 * d + 2, :]
        q = q_ref[0, sl, :]
        k = k_ref[0, sl, :] * (DQK_A ** -0.5)
        v = v_ref[0, sl, :].astype(BF)
        mask = (col >= row) if reverse else (col <= row)
        mask_t = (row >= col) if reverse else (row <= col)
        lf_b = jnp.broadcast_to(lf_row, (L, L))
        b_col = jnp.sum(jnp.where(mask, lf_b, 0.0), axis=1, keepdims=True)
        lf_col = jnp.sum(jnp.where(eye, lf_b, 0.0), axis=1, keepdims=True)
        b_row = jnp.sum(jnp.where(mask_t, lf_col, 0.0), axis=0, keepdims=True)
        ic_col = to_col(ic_row)
        dmat = jnp.where(mask, b_col - b_row + ic_row, NEG)
        inter = b_col + m
        m_t = jnp.maximum(inter, jnp.max(dmat, axis=1, keepdims=True))
        a = jnp.exp(inter - m_t)
        qb = q.astype(BF)
        s = lax.dot_general(qb, k.astype(BF), nt, preferred_element_type=F32) * jnp.exp(dmat - m_t)
        c_old = c_sc[d]
        num = a * jnp.dot(qb, c_old.astype(BF), preferred_element_type=F32)
        num += jnp.dot(s.astype(BF), v, preferred_element_type=F32)
        den = a * jnp.sum(q * n, axis=1, keepdims=True) + jnp.sum(s, axis=1, keepdims=True)
        h = num / jnp.maximum(jnp.abs(den), jnp.exp(-m_t))
        b_end = jnp.sum(lf_row, axis=1, keepdims=True)
        g_col = b_end - b_col + ic_col
        m_new = jnp.maximum(b_end + m, jnp.max(g_col, axis=0, keepdims=True))
        a_s = jnp.exp(b_end + m - m_new)
        kw = k * jnp.exp(g_col - m_new)
        c_sc[d] = a_s * c_old + lax.dot_general(kw.astype(BF), v, tn, preferred_element_type=F32)
        n_new = a_s * n + jnp.sum(kw, axis=0, keepdims=True)
        return h, sl, n_new, m_new

    c_sc[0] = c0_ref[0, 0, 0]
    c_sc[1] = c0_ref[0, 1, 0]
    init = (nm0_ref[0, 0, 0, 0:1, :], nm0_ref[0, 0, 0, 1:2, 0:1],
            nm0_ref[0, 1, 0, 0:1, :], nm0_ref[0, 1, 0, 1:2, 0:1])

    def body(i, carry):
        n_f, m_f, n_b, m_b = carry
        h, sl, n_f, m_f = chunk(i, 0, n_f, m_f, False)
        hf_sc[sl, :] = h
        h, sl, n_b, m_b = chunk(nc - 1 - i, 1, n_b, m_b, True)
        hb_sc[sl, :] = h
        return n_f, m_f, n_b, m_b

    n_f, m_f, n_b, m_b = lax.fori_loop(0, nc, body, init, unroll=2)

    cn_ref[0, 0, 0] = c_sc[0]
    cn_ref[0, 1, 0] = c_sc[1]
    for d, (n_d, m_d) in enumerate(((n_f, m_f), (n_b, m_b))):
        nmn_ref[0, d, 0] = jnp.concatenate(
            [n_d, jnp.broadcast_to(m_d, (1, DQK_A)), jnp.zeros((6, DQK_A), F32)], axis=0)

    rows = 256
    gain = ghn_ref[0]

    def out_body(i, _):
        sl = pl.ds(pl.multiple_of(i * rows, rows), rows)
        hs = hf_sc[sl, :] + hb_sc[sl, :]
        hn = hs * lax.rsqrt(jnp.mean(hs * hs, axis=-1, keepdims=True) + EPS) * gain
        gate = 1.0 / (1.0 + jnp.exp(-og_ref[0, sl, :]))
        y_ref[0, sl, :] = (gate * hn).astype(y_ref.dtype)
        return 0

    lax.fori_loop(0, (nc * L) // rows, out_body, 0)


def mlstm(proj, gates, c0, nm0, g_hn, *, nb, t_len, boff):
    nc = t_len // CHUNK
    kq = A_QK // DQK_A
    kv = 2 * A_QK // DV_A
    ko = (2 * A_QK + A_V) // DV_A
    return pl.pallas_call(
        functools.partial(_mlstm_kernel, nc=nc),
        grid=(nb, NH_A),
        in_specs=[
            pl.BlockSpec((1, t_len, DQK_A), lambda b, h: (boff + b, 0, h)),
            pl.BlockSpec((1, t_len, DQK_A), lambda b, h: (boff + b, 0, kq + h)),
            pl.BlockSpec((1, t_len, DV_A), lambda b, h: (boff + b, 0, kv + h)),
            pl.BlockSpec((1, t_len, DV_A), lambda b, h: (boff + b, 0, ko + h)),
            pl.BlockSpec((1, 1, nc, 4, CHUNK), lambda b, h: (b, h, 0, 0, 0)),
            pl.BlockSpec((1, 2, 1, DQK_A, DV_A), lambda b, h: (b, 0, h, 0, 0)),
            pl.BlockSpec((1, 2, 1, 8, DQK_A), lambda b, h: (b, 0, h, 0, 0)),
            pl.BlockSpec((1, 1, DV_A), lambda b, h: (h, 0, 0)),
        ],
        out_specs=[
            pl.BlockSpec((1, t_len, DV_A), lambda b, h: (b, 0, h)),
            pl.BlockSpec((1, 2, 1, DQK_A, DV_A), lambda b, h: (b, 0, h, 0, 0)),
            pl.BlockSpec((1, 2, 1, 8, DQK_A), lambda b, h: (b, 0, h, 0, 0)),
        ],
        out_shape=[
            jax.ShapeDtypeStruct((nb, t_len, A_V), BF),
            jax.ShapeDtypeStruct((nb, 2, NH_A, DQK_A, DV_A), F32),
            jax.ShapeDtypeStruct((nb, 2, NH_A, 8, DQK_A), F32),
        ],
        scratch_shapes=[pltpu.VMEM((t_len, DV_A), F32), pltpu.VMEM((t_len, DV_A), F32),
                        pltpu.VMEM((2, DQK_A, DV_A), F32)],
        compiler_params=_cparams(("parallel", "parallel")), name="mlstm",
    )(proj, proj, proj, proj, gates, c0, nm0, g_hn.reshape(NH_A, 1, DV_A))


def _w1_prep_kernel(w_ref, p_ref, o_ref):
    gw = 2 * MOE_GRP
    for c in range(w_ref.shape[2] // gw):
        blk = w_ref[0, :, c * gw:(c + 1) * gw].astype(BF)
        o_ref[0, :, c * gw:(c + 1) * gw] = jnp.dot(blk, p_ref[...], preferred_element_type=F32).astype(BF)


def _deinterleave_perm():
    gw = 2 * MOE_GRP
    j = np.arange(gw)
    src = np.where(j < MOE_GRP, 2 * j, 2 * (j - MOE_GRP) + 1)
    perm = np.zeros((gw, gw), np.float32)
    perm[src, j] = 1.0
    return jnp.asarray(perm, BF)


def w1_prep(w1):
    ne, d, n2 = w1.shape
    tn = 1024
    gw = 2 * MOE_GRP
    return pl.pallas_call(
        _w1_prep_kernel,
        grid=(ne, n2 // tn),
        in_specs=[pl.BlockSpec((1, d, tn), lambda e, j: (e, 0, j)), pl.BlockSpec((gw, gw), lambda e, j: (0, 0))],
        out_specs=pl.BlockSpec((1, d, tn), lambda e, j: (e, 0, j)),
        out_shape=jax.ShapeDtypeStruct((ne, d, n2), BF),
        compiler_params=_cparams(("parallel", "parallel")), name="w1_prep",
    )(w1, _deinterleave_perm())


def _cast_kernel(x_ref, o_ref):
    o_ref[...] = x_ref[...].astype(o_ref.dtype)


def cast_bf16(x, rows):
    ne, r, c = x.shape
    return pl.pallas_call(
        _cast_kernel,
        grid=(ne, r // rows),
        in_specs=[pl.BlockSpec((1, rows, c), lambda e, j: (e, j, 0))],
        out_specs=pl.BlockSpec((1, rows, c), lambda e, j: (e, j, 0)),
        out_shape=jax.ShapeDtypeStruct(x.shape, BF),
        compiler_params=_cparams(("parallel", "parallel")), name="cast_bf16",
    )(x)


def _moe_kernel(blk_e_ref, nused_ref, x_ref, w1_ref, b1_ref, w2_ref, b2_ref, o_ref):
    b = pl.program_id(0)
    f = pl.program_id(1)

    @pl.when(f == 0)
    def _():
        o_ref[...] = jnp.broadcast_to(b2_ref[0], o_ref.shape)

    @pl.when(b < nused_ref[0])
    def _():
        x = _unpack_bf16_pairs(x_ref[...])
        gw = 2 * MOE_GRP
        part = None
        for c in range(MOE_TF // MOE_GRP):
            hh = jnp.dot(x, w1_ref[0, :, c * gw:(c + 1) * gw], preferred_element_type=F32)
            hh = hh + b1_ref[0, :, c * gw:(c + 1) * gw]
            glu = jnp.minimum(hh[:, :MOE_GRP], SWIGLU_LIMIT)
            lin = jnp.clip(hh[:, MOE_GRP:], -SWIGLU_LIMIT, SWIGLU_LIMIT)
            act = glu * (1.0 / (1.0 + jnp.exp(-SWIGLU_ALPHA * glu))) * (lin + 1.0)
            p = jnp.dot(act.astype(BF), w2_ref[0, c * MOE_GRP:(c + 1) * MOE_GRP, :], preferred_element_type=F32)
            part = p if part is None else part + p
        o_ref[...] += part


def moe_experts(xb, blk_e, nused, w1p, b1p, w2, b2):
    nf = D_FF // MOE_TF
    d = D_MODEL

    def eidx(b, be):
        return be[b]

    def fidx(b, f, nu):
        return jnp.where(b < nu[0], f, nf - 1)

    grid_spec = pltpu.PrefetchScalarGridSpec(
        num_scalar_prefetch=2,
        grid=(MOE_NB, nf),
        in_specs=[
            pl.BlockSpec((MOE_BM, d // 2), lambda b, f, be, nu: (jnp.minimum(b, nu[0] - 1), 0)),
            pl.BlockSpec((1, d, 2 * MOE_TF), lambda b, f, be, nu: (eidx(b, be), 0, fidx(b, f, nu))),
            pl.BlockSpec((1, 1, 2 * MOE_TF), lambda b, f, be, nu: (eidx(b, be), 0, fidx(b, f, nu))),
            pl.BlockSpec((1, MOE_TF, d), lambda b, f, be, nu: (eidx(b, be), fidx(b, f, nu), 0)),
            pl.BlockSpec((1, 1, d), lambda b, f, be, nu: (eidx(b, be), 0, 0)),
        ],
        out_specs=pl.BlockSpec((MOE_BM, d), lambda b, f, be, nu: (b, 0)),
    )
    return pl.pallas_call(
        _moe_kernel,
        grid_spec=grid_spec,
        out_shape=jax.ShapeDtypeStruct((MOE_NB * MOE_BM, d), F32),
        compiler_params=_cparams(("arbitrary", "arbitrary")), name="moe_experts",
    )(blk_e, nused, xb, w1p, b1p, w2, b2)


def moe(hdn, logits, w1, b1, w2, b2):
    n = hdn.shape[0]
    nk = n * TOP_K
    top_v, top_i = lax.top_k(logits, TOP_K)
    gate = jax.nn.softmax(top_v, axis=-1)
    e = top_i.reshape(-1).astype(jnp.int32)
    rb = 512
    onehot = (e[None, :] == jnp.arange(N_EXP, dtype=jnp.int32)[:, None]).astype(BF).reshape(N_EXP, nk // rb, rb)
    within = jnp.einsum("ebs,ts->ebt", onehot, jnp.tril(jnp.ones((rb, rb), BF)), preferred_element_type=F32)
    tot = within[:, :, -1]
    before = jnp.cumsum(tot, axis=1) - tot
    rank = jnp.sum((within + before[:, :, None]) * onehot.astype(F32), axis=0).reshape(nk).astype(jnp.int32) - 1
    counts = (before[:, -1] + tot[:, -1]).astype(jnp.int32)
    padded = (counts + MOE_BM - 1) // MOE_BM * MOE_BM
    pad_end = jnp.cumsum(padded)
    pad_start = pad_end - padded
    dest = pad_start[e] + rank
    tok = jnp.arange(nk, dtype=jnp.int32) // TOP_K
    slot_tok = (jnp.arange(MOE_NB * MOE_BM, dtype=jnp.int32) % n).at[dest].set(
        tok, unique_indices=True, mode="promise_in_bounds")
    blk_start = jnp.arange(MOE_NB, dtype=jnp.int32) * MOE_BM
    blk_e = jnp.minimum(jnp.sum((pad_end[None, :] <= blk_start[:, None]).astype(jnp.int32), axis=1), N_EXP - 1)
    nused = (pad_end[-1] // MOE_BM).astype(jnp.int32).reshape(1)
    xb = hdn.at[slot_tok].get(mode="promise_in_bounds")
    b1p = b1.reshape(N_EXP, D_FF // MOE_GRP, MOE_GRP, 2).transpose(0, 1, 3, 2).reshape(N_EXP, 1, 2 * D_FF)
    yb = moe_experts(xb, blk_e, nused, w1_prep(w1), b1p, cast_bf16(w2, 1024), b2[:, None, :])
    return yb, dest.reshape(n, TOP_K), gate


def moe_combine(yb, dest, gate, lo, hi):
    y = None
    for k in range(TOP_K):
        yk = yb.at[dest[lo:hi, k]].get(mode="promise_in_bounds") * gate[lo:hi, k:k + 1]
        y = yk if y is None else y + yk
    return y


def _pad_cols(w, width):
    return jnp.pad(w, ((0, 0), (0, width - w.shape[1])))


def _adaln(cond, ada_w, ada_b):
    a = jnp.pad(jax.nn.silu(cond), ((0, 8 - cond.shape[0]), (0, 0)))
    mod = matmul(a, ada_w, bias=ada_b, tm=8, tn=1024)
    return mod[:cond.shape[0]].reshape(cond.shape[0], 6, D_MODEL)


def _layer0_mixer(x, mod, g_mix, w_in, b_gate, g_qa, w_qb, g_kva, w_kvb, g_qn, g_kn, g_hn, w_out, cache):
    st_c, st_n, st_m, c_ckv, c_kpe = cache
    o_gate = 2 * A_QK + 2 * A_V
    o_cq = o_gate + N_GATE
    w_gate = _pad_cols(w_in[:, o_gate:o_cq], AUX_W)
    hdn, gates = norm_mod(x, g_mix, mod, 0, aux_w=w_gate, aux_b=jnp.pad(b_gate, (0, AUX_W - N_GATE)))
    proj = matmul(hdn, w_in[:, :o_gate].astype(BF))
    w_b = _pad_cols(w_in[:, o_cq:], MLA_IN_W).astype(BF)
    proj_b = matmul(hdn, w_b, tn=MLA_IN_W)

    gt = GATE_CAP * jnp.tanh(gates[:, :N_GATE] / GATE_CAP)
    gt = gt.reshape(N_TOK, 4, NH_A)
    gt = jnp.stack([gt[:, 0], jax.nn.log_sigmoid(gt[:, 1]), gt[:, 2], jax.nn.log_sigmoid(gt[:, 3])], axis=1)

    def gate_layout(g, nb, t_len):
        g = g.reshape(nb, t_len // CHUNK, CHUNK, 4, NH_A)
        return g.transpose(0, 4, 1, 3, 2)

    zc = jnp.zeros((N_CTX_B, 2, NH_A, DQK_A, DV_A), F32)
    znm = jnp.zeros((N_CTX_B, 2, NH_A, 8, DQK_A), F32)
    y_ctx, new_c, new_nm = mlstm(proj.reshape(N_TOK // CTX_T, CTX_T, -1), gate_layout(gt[:N_CTX], N_CTX_B, CTX_T),
                                 zc, znm, g_hn, nb=N_CTX_B, t_len=CTX_T, boff=0)
    nm_lat = jnp.concatenate([st_n[:, :, :, None, :],
                              jnp.broadcast_to(st_m[:, :, :, None, None], (N_LAT_B, 2, NH_A, 1, DQK_A)),
                              jnp.zeros((N_LAT_B, 2, NH_A, 6, DQK_A), F32)], axis=3)
    y_lat, _, _ = mlstm(proj.reshape(N_TOK // LAT_T, LAT_T, -1), gate_layout(gt[N_CTX:], N_LAT_B, LAT_T),
                        st_c, nm_lat, g_hn, nb=N_LAT_B, t_len=LAT_T, boff=1)
    y_a = jnp.concatenate([y_ctx.reshape(N_CTX, A_V), y_lat.reshape(N_LAT, A_V)], axis=0)
    new_n = new_nm[:, :, :, 0, :]
    new_m = new_nm[:, :, :, 1, 0]

    wq = w_qb.reshape(Q_LORA, NH_B, QK_B)
    wq = jnp.pad(wq, ((0, 0), (0, 0), (0, QK_B_PAD - QK_B))).reshape(Q_LORA, NH_B * QK_B_PAD).astype(BF)
    gq = jnp.tile(jnp.pad(g_qn, (0, QK_B_PAD - QK_B)), NH_B)
    tables = _rotary_tables()
    w_kvb_b = w_kvb.astype(BF)
    q = mla_q(proj_b, g_qa, wq, gq, tables)
    k_new, v_new, ckv_n = mla_kv(proj_b, 1, proj_b, (Q_LORA + KV_LORA) // 128, g_kva, w_kvb_b, g_kn, tables,
                                 normalize=True)
    k_old, v_old = mla_kv(c_ckv.reshape(N_LAT_B * PAST, KV_LORA), 0,
                          _pad_cols(c_kpe.reshape(N_LAT_B * PAST, ROPE), 128), 0, g_kva, w_kvb_b, g_kn, None,
                          normalize=False)
    kpe = proj_b[:N_CTX, Q_LORA + KV_LORA:Q_LORA + KV_LORA + ROPE]
    ident = lambda b, h: (b, h)
    lat = lambda b, h: (1 + b, h)
    kw, vw = NH_B * QK_B_PAD, NH_B * VH
    y_b_ctx = attention(
        q.reshape(N_TOK // CTX_T, CTX_T, kw), ident,
        [(k_new.reshape(N_TOK // CTX_T, CTX_T, kw), ident, v_new.reshape(N_TOK // CTX_T, CTX_T, vw), ident, CTX_T)],
        nb=N_CTX_B, nh=NH_B, tq_len=CTX_T, dk=QK_B_PAD, dv=VH, tq=CTX_T, q_scale=None)
    y_b_lat = attention(
        q.reshape(N_TOK // LAT_T, LAT_T, kw), lat,
        [(k_new.reshape(N_TOK // LAT_T, LAT_T, kw), lat, v_new.reshape(N_TOK // LAT_T, LAT_T, vw), lat, LAT_T),
         (k_old.reshape(N_LAT_B, PAST, kw), ident, v_old.reshape(N_LAT_B, PAST, vw), ident, PAST)],
        nb=N_LAT_B, nh=NH_B, tq_len=LAT_T, dk=QK_B_PAD, dv=VH, tq=256, q_scale=None)
    y_b = jnp.concatenate([y_b_ctx.reshape(N_CTX, -1), y_b_lat.reshape(N_LAT, -1)], axis=0)

    y = jnp.concatenate([y_a, y_b], axis=1)
    x = matmul(y, w_out.astype(BF), resid=x, mod=mod, gate_row=2)
    new = (new_c, new_n, new_m, ckv_n[:N_CTX].reshape(N_CTX_B, CTX_T, KV_LORA), kpe.reshape(N_CTX_B, CTX_T, ROPE))
    return x, new


def _layer1_mixer(x, mod, g_mix, w_qkv, g_qn, g_kn, rpb, w_out, cache):
    c_k, c_v = cache
    hd = NH_C * DH_C
    hdn = norm_mod(x, g_mix, mod, 0)
    gain = jnp.concatenate([jnp.tile(g_qn, NH_C), jnp.tile(g_kn, NH_C), jnp.ones((hd,), F32)])
    qkv = matmul(hdn, w_qkv.astype(BF), gain=gain, norm_group=DH_C, norm_div=float(DH_C), n_norm_cols=2 * hd)
    new_k = qkv[:N_CTX, hd:2 * hd].reshape(N_CTX_B, CTX_T, NH_C, DH_C)
    new_v = qkv[:N_CTX, 2 * hd:].reshape(N_CTX_B, CTX_T, NH_C, DH_C)
    qkv_c = qkv.reshape(N_TOK // CTX_T, CTX_T, 3 * hd)
    o_ctx = attention(
        qkv_c, lambda b, h: (b, h),
        [(qkv_c, lambda b, h: (b, NH_C + h), qkv_c, lambda b, h: (b, 2 * NH_C + h), CTX_T)],
        nb=N_CTX_B, nh=NH_C, tq_len=CTX_T, dk=DH_C, dv=DH_C, tq=CTX_T, q_scale=DH_C ** -0.5)
    o_lat = natten(qkv.reshape(N_TOK // LAT_T, LAT_T, 3 * hd), c_k.reshape(N_LAT_B, PAST, hd),
                   c_v.reshape(N_LAT_B, PAST, hd), rpb)
    o = jnp.concatenate([o_ctx.reshape(N_CTX, hd), o_lat.reshape(N_LAT, hd)], axis=0)
    x = matmul(o, w_out.astype(BF), resid=x, mod=mod, gate_row=2)
    return x, (new_k, new_v)


def _moe_sublayer(x, mod, g_moe, w_r, b_r, w1, b1, w2, b2, split=False):
    hdn, logits = norm_mod(x, g_moe, mod, 1, aux_w=_pad_cols(w_r, AUX_W), aux_b=jnp.pad(b_r, (0, AUX_W - N_EXP)),
                           packed=True)
    yb, dest, gate = moe(hdn, logits[:, :N_EXP], w1, b1, w2, b2)
    ga = mod[:, 5, :]
    if not split:
        ga_rows = jnp.repeat(ga, N_CTX, axis=0, total_repeat_length=N_TOK)
        return x + ga_rows * moe_combine(yb, dest, gate, 0, N_TOK)
    y_ctx = x[:N_CTX] + ga[0][None, :] * moe_combine(yb, dest, gate, 0, N_CTX)
    y_lat = x[N_CTX:].reshape(N_LAT_B, LAT_T, D_MODEL) + ga[1:, None, :] * moe_combine(
        yb, dest, gate, N_CTX, N_TOK).reshape(N_LAT_B, LAT_T, D_MODEL)
    return y_ctx, y_lat.reshape(N_LAT, D_MODEL)


def kernel(x_prompt, x_sample, state_l0_mlstm_C, state_l0_mlstm_n, state_l0_mlstm_m, cache_l0_mla_ckv, cache_l0_mla_kpe, cache_l1_na_k, cache_l1_na_v, c, c_ctx, l0_g_mix, l0_g_moe, l0_ada_w, l0_ada_b, l0_w_in, l0_b_gate, l0_g_qa, l0_w_qb, l0_g_kva, l0_w_kvb, l0_g_qn, l0_g_kn, l0_g_hn, l0_w_out, l0_w_router, l0_b_router, l0_w1, l0_b1, l0_w2, l0_b2, l1_g_mix, l1_g_moe, l1_ada_w, l1_ada_b, l1_w_qkv, l1_g_qn, l1_g_kn, l1_rpb, l1_w_out, l1_w_router, l1_b_router, l1_w1, l1_b1, l1_w2, l1_b2):
    x = jnp.concatenate([x_prompt.reshape(N_CTX, D_MODEL), x_sample.reshape(N_LAT, D_MODEL)], axis=0)
    cond = jnp.concatenate([c_ctx[None, :], c], axis=0)

    mod0 = _adaln(cond, l0_ada_w, l0_ada_b)
    x, (new_c, new_n, new_m, new_ckv, new_kpe) = _layer0_mixer(
        x, mod0, l0_g_mix, l0_w_in, l0_b_gate, l0_g_qa, l0_w_qb, l0_g_kva, l0_w_kvb, l0_g_qn, l0_g_kn, l0_g_hn,
        l0_w_out, (state_l0_mlstm_C, state_l0_mlstm_n, state_l0_mlstm_m, cache_l0_mla_ckv, cache_l0_mla_kpe))
    x = _moe_sublayer(x, mod0, l0_g_moe, l0_w_router, l0_b_router, l0_w1, l0_b1, l0_w2, l0_b2)

    mod1 = _adaln(cond, l1_ada_w, l1_ada_b)
    x, (new_k, new_v) = _layer1_mixer(x, mod1, l1_g_mix, l1_w_qkv, l1_g_qn, l1_g_kn, l1_rpb, l1_w_out,
                                      (cache_l1_na_k, cache_l1_na_v))
    y_ctx, y_lat = _moe_sublayer(x, mod1, l1_g_moe, l1_w_router, l1_b_router, l1_w1, l1_b1, l1_w2, l1_b2, split=True)

    y_prompt = y_ctx.reshape(N_CTX_B, CTX_T, D_MODEL)
    y_sample = y_lat.reshape(N_LAT_B, LAT_T, D_MODEL)
    return (y_prompt, y_sample, new_c, new_n, new_m, new_ckv, new_kpe, new_k, new_v)
```
